```python
import math
import jax, jax.numpy as jnp
from jax import lax
import numpy as np

D_MODEL = 2048
BATCH = 2
SEQ = 4096
DEPTH = 2
DEC_BATCH = 128
DEC_SEQ = 1
PAST_LEN = 16384
PAGE_SIZE = 128

N_A_LAYERS = DEPTH // 2
N_B_LAYERS = DEPTH - N_A_LAYERS
MLA_HEADS = 16
Q_LORA = 512
KV_LORA = 512
NOPE_DIM = 128
ROPE_DIM = 64
V_DIM = 128
ROPE_THETA = 10000.0
MLA_SCALE = (NOPE_DIM + ROPE_DIM) ** -0.5
DIL_GROUPS = ((128, 1), (512, 4), (2048, 16))
N_GROUPS = len(DIL_GROUPS)
GROUP_HEADS = 6
B_HEAD_DIM = 128
B_HEADS = N_GROUPS * GROUP_HEADS
B_SCALE = B_HEAD_DIM ** -0.5
T5_BUCKETS = 32
T5_MAX_DIST = 2048
D_FF = -(-8 * D_MODEL // (3 * 256)) * 256
Q_BLOCK = 128
EPS = 1e-6
NEG_INF = -1e30

kernel_name = "yoco_mla_dilated_decode_step"


def rmsnorm(x, g):
    xf = x.astype(jnp.float32)
    y = xf * lax.rsqrt(jnp.mean(xf * xf, axis=-1, keepdims=True) + EPS)
    return (y * g.astype(jnp.float32)).astype(x.dtype)


def rope(x, pos):
    half = ROPE_DIM // 2
    inv = ROPE_THETA ** (-jnp.arange(half, dtype=jnp.float32) / half)
    ang = pos.astype(jnp.float32)[:, None] * inv
    ang = ang.reshape(ang.shape[:1] + (1,) * (x.ndim - 3) + ang.shape[1:])
    cos, sin = jnp.cos(ang).astype(x.dtype), jnp.sin(ang).astype(x.dtype)
    x1, x2 = x[..., :half], x[..., half:]
    return jnp.concatenate([x1 * cos - x2 * sin, x1 * sin + x2 * cos], axis=-1)


def t5_bucket(dist):
    max_exact = T5_BUCKETS // 2
    d = np.maximum(dist, 1).astype(np.float32)
    large = max_exact + (np.log(d / max_exact) / math.log(T5_MAX_DIST / max_exact)
                         * (T5_BUCKETS - max_exact)).astype(np.int32)
    large = np.minimum(large, T5_BUCKETS - 1)
    return np.where(dist < max_exact, dist, large).astype(np.int32)


def group_bias(t5_bias, g):
    window, dil = DIL_GROUPS[g]
    buckets = t5_bucket(dil * np.arange(window // dil + 1))
    return t5_bias[buckets, g * GROUP_HEADS:(g + 1) * GROUP_HEADS].astype(jnp.float32)


def swiglu(x, w_in, w_out):
    gate, up = jnp.split(x @ w_in, 2, axis=-1)
    return (jax.nn.silu(gate) * up) @ w_out


def mla_project(hn, pos, w_in, g_q, g_kv, w_uq):
    a = hn @ w_in
    cq = rmsnorm(a[..., :Q_LORA], g_q)
    c = rmsnorm(a[..., Q_LORA:Q_LORA + KV_LORA], g_kv)
    k_rope = rope(a[..., Q_LORA + KV_LORA:], pos)
    q = (cq @ w_uq).reshape(cq.shape[0], cq.shape[1], MLA_HEADS, NOPE_DIM + ROPE_DIM)
    q_nope, q_rope = q[..., :NOPE_DIM], rope(q[..., NOPE_DIM:], pos)
    rows = jnp.concatenate([c, k_rope], axis=-1)
    return q_nope, q_rope, rows


def mla_prompt(q_nope, q_rope, rows, w_uk, w_uv, w_o):
    B, S = rows.shape[:2]
    c, k_rope = rows[..., :KV_LORA], rows[..., KV_LORA:]
    k_nope = jnp.einsum('bsc,chn->bshn', c, w_uk)
    v = jnp.einsum('bsc,chv->bshv', c, w_uv)
    nb = S // Q_BLOCK
    key_pos = jnp.arange(S)

    def blk(args):
        qn, qr, s0 = args
        logits = (jnp.einsum('bqhn,bkhn->bhqk', qn, k_nope)
                  + jnp.einsum('bqhr,bkr->bhqk', qr, k_rope)).astype(jnp.float32) * MLA_SCALE
        mask = key_pos[None, :] <= (s0 + jnp.arange(Q_BLOCK))[:, None]
        p = jax.nn.softmax(jnp.where(mask, logits, NEG_INF), axis=-1)
        return jnp.einsum('bhqk,bkhv->bqhv', p.astype(v.dtype), v)

    qn_b = q_nope.reshape(B, nb, Q_BLOCK, MLA_HEADS, NOPE_DIM).swapaxes(0, 1)
    qr_b = q_rope.reshape(B, nb, Q_BLOCK, MLA_HEADS, ROPE_DIM).swapaxes(0, 1)
    o = lax.map(blk, (qn_b, qr_b, jnp.arange(nb) * Q_BLOCK))
    return o.swapaxes(0, 1).reshape(B, S, MLA_HEADS * V_DIM) @ w_o


def mla_sample(q_nope, q_rope, rows_new, past, w_uk, w_uv, w_o):
    DB, T = rows_new.shape[:2]
    P = past.shape[1]
    q_lat = jnp.einsum('bthn,chn->bthc', q_nope, w_uk)
    q_cat = jnp.concatenate([q_lat, q_rope], axis=-1)
    logits = jnp.concatenate([jnp.einsum('bthe,bke->bhtk', q_cat, past),
                              jnp.einsum('bthe,bke->bhtk', q_cat, rows_new)], axis=-1)
    logits = logits.astype(jnp.float32) * MLA_SCALE
    causal = jnp.arange(T)[None, :] <= jnp.arange(T)[:, None]
    mask = jnp.concatenate([jnp.ones((T, P), bool), causal], axis=-1)
    p = jax.nn.softmax(jnp.where(mask, logits, NEG_INF), axis=-1).astype(past.dtype)
    o = (jnp.einsum('bhtk,bke->bthe', p[..., :P], past)
         + jnp.einsum('bhtk,bke->bthe', p[..., P:], rows_new))
    o = jnp.einsum('bthc,chv->bthv', o[..., :KV_LORA], w_uv)
    return o.reshape(DB, T, MLA_HEADS * V_DIM) @ w_o


def dilated_attend(q, slab, idx, valid, bias):
    kv = slab[:, idx]
    k, v = kv[:, :, :, 0], kv[:, :, :, 1]
    logits = jnp.einsum('bqhd,bqkhd->bhqk', q, k).astype(jnp.float32) * B_SCALE + bias.T[None, :, None, :]
    logits = jnp.where(valid[None, None], logits, NEG_INF)
    m = jnp.max(logits, axis=-1, keepdims=True)
    p = jnp.exp(logits - m)
    s = jnp.sum(p, axis=-1)
    o = jnp.einsum('bhqk,bqkhd->bqhd', p, v.astype(jnp.float32)) / jnp.swapaxes(s, 1, 2)[..., None]
    lse = jnp.swapaxes(m[..., 0] + jnp.log(s), 1, 2)
    return o, lse


def dilated_prompt(q, kv, window, dil, bias):
    B, S = q.shape[:2]
    nb = S // Q_BLOCK
    kvp = jnp.pad(kv, ((0, 0), (window, 0), (0, 0), (0, 0), (0, 0)))
    rel = np.arange(Q_BLOCK)[:, None] - dil * np.arange(window // dil + 1)[None, :]

    def blk(args):
        q_blk, s0 = args
        slab = lax.dynamic_slice_in_dim(kvp, s0, window + Q_BLOCK, axis=1)
        return dilated_attend(q_blk, slab, rel + window, (s0 + rel) >= 0, bias)

    qb = q.reshape(B, nb, Q_BLOCK, GROUP_HEADS, B_HEAD_DIM).swapaxes(0, 1)
    o, lse = lax.map(blk, (qb, jnp.arange(nb) * Q_BLOCK))
    return (o.swapaxes(0, 1).reshape(B, S, GROUP_HEADS, B_HEAD_DIM),
            lse.swapaxes(0, 1).reshape(B, S, GROUP_HEADS))


def dilated_sample(q, kv_new, buf, window, dil, bias):
    L, T = buf.shape[1], q.shape[1]
    ext = jnp.concatenate([buf, kv_new], axis=1)
    slab = jnp.pad(ext, ((0, 0), (window - L, 0), (0, 0), (0, 0), (0, 0)))
    rel = np.arange(T)[:, None] - dil * np.arange(window // dil + 1)[None, :]
    return dilated_attend(q, slab, rel + window, rel >= -L, bias)


def merge_groups(outs, lses):
    w = jax.nn.softmax(jnp.stack(lses, axis=0), axis=0)
    return jnp.sum(w[..., None] * jnp.stack(outs, axis=0), axis=0)


def setup_inputs(seed: int = 0) -> dict:
    key = jax.random.key(seed)
    ks = jax.random.split(key, 24)
    f32 = jnp.float32
    n_pages = PAST_LEN // PAGE_SIZE
    n_pool = (5 * DEC_BATCH * n_pages + 3) // 4

    def w(k, shape, fan):
        return jax.random.normal(k, shape, f32) * fan ** -0.5

    def gain(k, shape):
        return 1.0 + 0.05 * jax.random.normal(k, shape, f32)

    page_table = jax.random.permutation(ks[3], n_pool)[:DEC_BATCH * n_pages]
    return {
        "x_prompt": jax.random.normal(ks[0], (BATCH, SEQ, D_MODEL), f32),
        "x_sample": jax.random.normal(ks[1], (DEC_BATCH, DEC_SEQ, D_MODEL), f32),
        "cache_mla": jax.random.normal(ks[2], (N_A_LAYERS, n_pool, PAGE_SIZE, KV_LORA + ROPE_DIM), f32),
        "page_table": page_table.reshape(DEC_BATCH, n_pages).astype(jnp.int32),
        "cache_b_win0": jax.random.normal(ks[4], (DEC_BATCH, min(DIL_GROUPS[0][0], PAST_LEN), 2, GROUP_HEADS, B_HEAD_DIM), f32),
        "cache_b_win1": jax.random.normal(ks[5], (DEC_BATCH, min(DIL_GROUPS[1][0], PAST_LEN), 2, GROUP_HEADS, B_HEAD_DIM), f32),
        "cache_b_win2": jax.random.normal(ks[6], (DEC_BATCH, min(DIL_GROUPS[2][0], PAST_LEN), 2, GROUP_HEADS, B_HEAD_DIM), f32),
        "norm_gains": gain(ks[7], (DEPTH, 4, D_MODEL)),
        "w_mla_in": w(ks[8], (N_A_LAYERS, D_MODEL, Q_LORA + KV_LORA + ROPE_DIM), D_MODEL),
        "g_q_a": gain(ks[9], (N_A_LAYERS, Q_LORA)),
        "g_kv_a": gain(ks[10], (N_A_LAYERS, KV_LORA)),
        "w_uq": w(ks[11], (N_A_LAYERS, Q_LORA, MLA_HEADS * (NOPE_DIM + ROPE_DIM)), Q_LORA),
        "w_uk": w(ks[12], (N_A_LAYERS, KV_LORA, MLA_HEADS, NOPE_DIM), KV_LORA),
        "w_uv": w(ks[13], (N_A_LAYERS, KV_LORA, MLA_HEADS, V_DIM), KV_LORA),
        "w_mla_o": w(ks[14], (N_A_LAYERS, MLA_HEADS * V_DIM, D_MODEL), MLA_HEADS * V_DIM),
        "g_kv_b": gain(ks[15], (D_MODEL,)),
        "w_kv_b": w(ks[16], (D_MODEL, 2 * B_HEADS * B_HEAD_DIM), D_MODEL),
        "w_q_b": w(ks[17], (N_B_LAYERS, D_MODEL, B_HEADS * B_HEAD_DIM), D_MODEL),
        "w_b_o": w(ks[18], (N_B_LAYERS, GROUP_HEADS * B_HEAD_DIM, D_MODEL), GROUP_HEADS * B_HEAD_DIM),
        "t5_bias": 0.5 * jax.random.normal(ks[19], (T5_BUCKETS, B_HEADS), f32),
        "w_ffn_in": w(ks[20], (DEPTH, D_MODEL, 2 * D_FF), D_MODEL),
        "w_ffn_out": w(ks[21], (DEPTH, D_FF, D_MODEL), D_FF),
    }


def reference(x_prompt, x_sample, cache_mla, page_table, cache_b_win0, cache_b_win1, cache_b_win2,
              norm_gains, w_mla_in, g_q_a, g_kv_a, w_uq, w_uk, w_uv, w_mla_o,
              g_kv_b, w_kv_b, w_q_b, w_b_o, t5_bias, w_ffn_in, w_ffn_out):
    xp, xs = x_prompt, x_sample
    S, DB, T = xp.shape[1], xs.shape[0], xs.shape[1]
    pos_p = jnp.arange(S)
    pos_s = PAST_LEN + jnp.arange(T)
    win_bufs = (cache_b_win0, cache_b_win1, cache_b_win2)
    mla_rows_p, mla_rows_s = [], []
    for l in range(DEPTH):
        g = norm_gains[l]
        if l < N_A_LAYERS:
            qn, qr, rows_p = mla_project(rmsnorm(xp, g[0]), pos_p, w_mla_in[l], g_q_a[l], g_kv_a[l], w_uq[l])
            mix_p = mla_prompt(qn, qr, rows_p, w_uk[l], w_uv[l], w_mla_o[l])
            qn, qr, rows_s = mla_project(rmsnorm(xs, g[0]), pos_s, w_mla_in[l], g_q_a[l], g_kv_a[l], w_uq[l])
            past = cache_mla[l, page_table].reshape(DB, -1, KV_LORA + ROPE_DIM)
            mix_s = mla_sample(qn, qr, rows_s, past, w_uk[l], w_uv[l], w_mla_o[l])
            mla_rows_p.append(rows_p)
            mla_rows_s.append(rows_s)
        else:
            j = l - N_A_LAYERS
            if j == 0:
                kv_p = (rmsnorm(xp, g_kv_b) @ w_kv_b).reshape(xp.shape[0], S, 2, N_GROUPS, GROUP_HEADS, B_HEAD_DIM)
                kv_s = (rmsnorm(xs, g_kv_b) @ w_kv_b).reshape(DB, T, 2, N_GROUPS, GROUP_HEADS, B_HEAD_DIM)
                kv_p_g = [kv_p[:, :, :, gi] for gi in range(N_GROUPS)]
                kv_s_g = [kv_s[:, :, :, gi] for gi in range(N_GROUPS)]
                biases = [group_bias(t5_bias, gi) for gi in range(N_GROUPS)]
            q_p = (rmsnorm(xp, g[0]) @ w_q_b[j]).reshape(xp.shape[0], S, N_GROUPS, GROUP_HEADS, B_HEAD_DIM)
            q_s = (rmsnorm(xs, g[0]) @ w_q_b[j]).reshape(DB, T, N_GROUPS, GROUP_HEADS, B_HEAD_DIM)
            op, lp, os_, ls = [], [], [], []
            for gi, (window, dil) in enumerate(DIL_GROUPS):
                o, lse = dilated_prompt(q_p[:, :, gi], kv_p_g[gi], window, dil, biases[gi])
                op.append(o)
                lp.append(lse)
                o, lse = dilated_sample(q_s[:, :, gi], kv_s_g[gi], win_bufs[gi], window, dil, biases[gi])
                os_.append(o)
                ls.append(lse)
            mix_p = merge_groups(op, lp).astype(xp.dtype).reshape(xp.shape[0], S, -1) @ w_b_o[j]
            mix_s = merge_groups(os_, ls).astype(xs.dtype).reshape(DB, T, -1) @ w_b_o[j]
        xp = xp + rmsnorm(mix_p, g[1])
        xs = xs + rmsnorm(mix_s, g[1])
        xp = xp + rmsnorm(swiglu(rmsnorm(xp, g[2]), w_ffn_in[l], w_ffn_out[l]), g[3])
        xs = xs + rmsnorm(swiglu(rmsnorm(xs, g[2]), w_ffn_in[l], w_ffn_out[l]), g[3])
    new_mla_p = jnp.stack(mla_rows_p, axis=0)
    new_mla_s = jnp.stack(mla_rows_s, axis=0)
    w0p = kv_p_g[0][:, S - min(DIL_GROUPS[0][0], S):]
    w1p = kv_p_g[1][:, S - min(DIL_GROUPS[1][0], S):]
    w2p = kv_p_g[2][:, S - min(DIL_GROUPS[2][0], S):]
    return (xp, xs, new_mla_p, new_mla_s, w0p, kv_s_g[0], w1p, kv_s_g[1], w2p, kv_s_g[2])
```

```python
import functools
import math

import numpy as np
import jax
import jax.numpy as jnp
from jax import lax
from jax.experimental import pallas as pl
from jax.experimental.pallas import tpu as pltpu

D_MODEL = 2048
DEPTH = 2
PAST_LEN = 16384
PAGE_SIZE = 128
N_A_LAYERS = DEPTH // 2
MLA_HEADS = 16
Q_LORA = 512
KV_LORA = 512
NOPE_DIM = 128
ROPE_DIM = 64
V_DIM = 128
ROPE_THETA = 10000.0
MLA_SCALE = (NOPE_DIM + ROPE_DIM) ** -0.5
DIL_GROUPS = ((128, 1), (512, 4), (2048, 16))
N_GROUPS = len(DIL_GROUPS)
GROUP_HEADS = 6
B_HEAD_DIM = 128
B_HEADS = N_GROUPS * GROUP_HEADS
B_SCALE = B_HEAD_DIM ** -0.5
T5_BUCKETS = 32
T5_MAX_DIST = 2048
EPS = 1e-6
NEG_INF = -1e30

F32 = jnp.float32
BF16 = jnp.bfloat16

VMEM_LIMIT_BYTES = 56 * 1024 * 1024
LANES = 128
HEAD_PAD = 2 * LANES
ROW_DIM = KV_LORA + ROPE_DIM
GROUP_DIM = GROUP_HEADS * B_HEAD_DIM
DIL_KEYS = 128
DIL_TILE = 128
DEC_PAGES = 16
DIL_SAMPLE_BLOCK = 8


def _cparams(*semantics):
    return pltpu.CompilerParams(dimension_semantics=semantics, vmem_limit_bytes=VMEM_LIMIT_BYTES)


def _rms(x, g):
    return x * lax.rsqrt(jnp.mean(x * x, axis=-1, keepdims=True) + EPS) * g


def _dot(a, b):
    return jnp.dot(a, b, preferred_element_type=F32)


def _dot_nt(a, b):
    return lax.dot_general(a, b, (((1,), (1,)), ((), ())), preferred_element_type=F32)


def _rope128(x, c, s):
    lane = lax.broadcasted_iota(jnp.int32, x.shape, 1)
    half = ROPE_DIM // 2
    swapped = jnp.where(lane < half, pltpu.roll(x, LANES - half, 1), pltpu.roll(x, half, 1))
    return x * c + swapped * s


def _mla_in_kernel(x_ref, g_ref, w_ref, gq_ref, gkv_ref, c_ref, s_ref, cq_ref, rows_ref):
    xn = _rms(x_ref[...], g_ref[...]).astype(BF16)
    a = _dot(xn, w_ref[...])
    cq_ref[...] = _rms(a[:, :Q_LORA], gq_ref[...]).astype(BF16)
    rows_ref[:, :KV_LORA] = _rms(a[:, Q_LORA:Q_LORA + KV_LORA], gkv_ref[...])
    kr = _rope128(a[:, Q_LORA + KV_LORA:], c_ref[...], s_ref[...])
    rows_ref[:, KV_LORA:] = kr[:, :ROPE_DIM]


def _mla_in(x, g, w_pad, gq, gkv, c_tab, s_tab, tm):
    m, d = x.shape
    n = w_pad.shape[1]
    return pl.pallas_call(
        _mla_in_kernel,
        grid=(m // tm,),
        in_specs=[
            pl.BlockSpec((tm, d), lambda i: (i, 0)),
            pl.BlockSpec((1, d), lambda i: (0, 0)),
            pl.BlockSpec((d, n), lambda i: (0, 0)),
            pl.BlockSpec((1, Q_LORA), lambda i: (0, 0)),
            pl.BlockSpec((1, KV_LORA), lambda i: (0, 0)),
            pl.BlockSpec((tm, LANES), lambda i: (i, 0)),
            pl.BlockSpec((tm, LANES), lambda i: (i, 0)),
        ],
        out_specs=[
            pl.BlockSpec((tm, Q_LORA), lambda i: (i, 0)),
            pl.BlockSpec((tm, ROW_DIM), lambda i: (i, 0)),
        ],
        out_shape=[
            jax.ShapeDtypeStruct((m, Q_LORA), BF16),
            jax.ShapeDtypeStruct((m, ROW_DIM), F32),
        ],
        compiler_params=_cparams("parallel"),
        name="mla_in",
    )(x, g, w_pad, gq, gkv, c_tab, s_tab)


def _q_up_kernel(cq_ref, w_ref, c_ref, s_ref, o_ref, *, heads_per_tile):
    a = _dot(cq_ref[...], w_ref[...])
    c = c_ref[...]
    s = s_ref[...]
    for h in range(heads_per_tile):
        lo = h * HEAD_PAD
        o_ref[:, lo:lo + LANES] = a[:, lo:lo + LANES].astype(BF16)
        o_ref[:, lo + LANES:lo + HEAD_PAD] = _rope128(a[:, lo + LANES:lo + HEAD_PAD], c, s).astype(BF16)


def _q_up(cq, w_pad, c_tab, s_tab, tm, tn):
    m, k = cq.shape
    n = w_pad.shape[1]
    return pl.pallas_call(
        functools.partial(_q_up_kernel, heads_per_tile=tn // HEAD_PAD),
        grid=(m // tm, n // tn),
        in_specs=[
            pl.BlockSpec((tm, k), lambda i, j: (i, 0)),
            pl.BlockSpec((k, tn), lambda i, j: (0, j)),
            pl.BlockSpec((tm, LANES), lambda i, j: (i, 0)),
            pl.BlockSpec((tm, LANES), lambda i, j: (i, 0)),
        ],
        out_specs=pl.BlockSpec((tm, tn), lambda i, j: (i, j)),
        out_shape=jax.ShapeDtypeStruct((m, n), BF16),
        compiler_params=_cparams("parallel", "parallel"),
        name="mla_q_up",
    )(cq, w_pad, c_tab, s_tab)


def _cast_mm_kernel(x_ref, w_ref, o_ref):
    o_ref[...] = _dot(x_ref[...].astype(BF16), w_ref[...]).astype(o_ref.dtype)


def _cast_mm(x, w, out_dtype, tm, tn):
    m, k = x.shape
    n = w.shape[1]
    return pl.pallas_call(
        _cast_mm_kernel,
        grid=(m // tm, n // tn),
        in_specs=[
            pl.BlockSpec((tm, k), lambda i, j: (i, 0)),
            pl.BlockSpec((k, tn), lambda i, j: (0, j)),
        ],
        out_specs=pl.BlockSpec((tm, tn), lambda i, j: (i, j)),
        out_shape=jax.ShapeDtypeStruct((m, n), out_dtype),
        compiler_params=_cparams("parallel", "parallel"),
        name="cast_mm",
    )(x, w)


def _norm_mm_kernel(x_ref, g_ref, w_ref, o_ref, xn_ref):
    @pl.when(pl.program_id(1) == 0)
    def _():
        xn_ref[...] = _rms(x_ref[...], g_ref[...]).astype(BF16)

    o_ref[...] = _dot(xn_ref[...], w_ref[...]).astype(o_ref.dtype)


def _norm_mm(x, g, w, out_dtype, tm, tn):
    m, d = x.shape
    n = w.shape[1]
    return pl.pallas_call(
        _norm_mm_kernel,
        grid=(m // tm, n // tn),
        in_specs=[
            pl.BlockSpec((tm, d), lambda i, j: (i, 0)),
            pl.BlockSpec((1, d), lambda i, j: (0, 0)),
            pl.BlockSpec((d, tn), lambda i, j: (0, j)),
        ],
        out_specs=pl.BlockSpec((tm, tn), lambda i, j: (i, j)),
        out_shape=jax.ShapeDtypeStruct((m, n), out_dtype),
        scratch_shapes=[pltpu.VMEM((tm, d), BF16)],
        compiler_params=_cparams("parallel", "arbitrary"),
        name="norm_mm",
    )(x, g, w)


def _mm_norm_res_kernel(a_ref, w_ref, x_ref, g_ref, o_ref, *, nk):
    k = pl.program_id(1)
    part = _dot(a_ref[...], w_ref[...])

    @pl.when(k == 0)
    def _():
        o_ref[...] = part

    @pl.when(k > 0)
    def _():
        o_ref[...] += part

    @pl.when(k == nk - 1)
    def _():
        o_ref[...] = x_ref[...] + _rms(o_ref[...], g_ref[...])


def _mm_norm_res(a, w, x, g, tm, tk):
    m, kdim = a.shape
    d = w.shape[1]
    nk = kdim // tk
    return pl.pallas_call(
        functools.partial(_mm_norm_res_kernel, nk=nk),
        grid=(m // tm, nk),
        in_specs=[
            pl.BlockSpec((tm, tk), lambda i, k: (i, k)),
            pl.BlockSpec((tk, d), lambda i, k: (k, 0)),
            pl.BlockSpec((tm, d), lambda i, k: (i, 0)),
            pl.BlockSpec((1, d), lambda i, k: (0, 0)),
        ],
        out_specs=pl.BlockSpec((tm, d), lambda i, k: (i, 0)),
        out_shape=jax.ShapeDtypeStruct((m, d), F32),
        compiler_params=_cparams("parallel", "arbitrary"),
        name="mm_norm_res",
    )(a, w, x, g)


def _ffn_kernel(x_ref, g_in_ref, g_out_ref, wg_ref, wu_ref, wo_ref, o_ref, xn_ref, *, nf):
    j = pl.program_id(1)

    @pl.when(j == 0)
    def _():
        xn_ref[...] = _rms(x_ref[...], g_in_ref[...]).astype(BF16)

    xn = xn_ref[...]
    gate = _dot(xn, wg_ref[...])
    up = _dot(xn, wu_ref[...])
    act = (gate * jax.nn.sigmoid(gate) * up).astype(BF16)
    part = _dot(act, wo_ref[...])

    @pl.when(j == 0)
    def _():
        o_ref[...] = part

    @pl.when(j > 0)
    def _():
        o_ref[...] += part

    @pl.when(j == nf - 1)
    def _():
        o_ref[...] = x_ref[...] + _rms(o_ref[...], g_out_ref[...])


def _ffn(x, g_in, g_out, w_in, w_out, tm, tf):
    m, d = x.shape
    f = w_out.shape[0]
    nf = f // tf
    return pl.pallas_call(
        functools.partial(_ffn_kernel, nf=nf),
        grid=(m // tm, nf),
        in_specs=[
            pl.BlockSpec((tm, d), lambda i, j: (i, 0)),
            pl.BlockSpec((1, d), lambda i, j: (0, 0)),
            pl.BlockSpec((1, d), lambda i, j: (0, 0)),
            pl.BlockSpec((d, tf), lambda i, j: (0, j)),
            pl.BlockSpec((d, tf), lambda i, j: (0, j + nf)),
            pl.BlockSpec((tf, d), lambda i, j: (j, 0)),
        ],
        out_specs=pl.BlockSpec((tm, d), lambda i, j: (i, 0)),
        out_shape=jax.ShapeDtypeStruct((m, d), F32),
        scratch_shapes=[pltpu.VMEM((tm, d), BF16)],
        compiler_params=_cparams("parallel", "arbitrary"),
        name="ffn",
    )(x, g_in, g_out, w_in, w_in, w_out)


def _flash_kernel(q_ref, k_ref, v_ref, o_ref, m_ref, l_ref, acc_ref, *, tq):
    qi = pl.program_id(2)
    q = q_ref[...]
    m_ref[...] = jnp.full(m_ref.shape, -jnp.inf, F32)
    l_ref[...] = jnp.zeros(l_ref.shape, F32)
    acc_ref[...] = jnp.zeros(acc_ref.shape, F32)

    def step(kc, diagonal):
        start = pl.multiple_of(kc * tq, tq)
        k = k_ref[pl.ds(start, tq), :]
        v = v_ref[pl.ds(start, tq), :]
        s = _dot_nt(q, k) * MLA_SCALE
        if diagonal:
            row = lax.broadcasted_iota(jnp.int32, s.shape, 0)
            col = lax.broadcasted_iota(jnp.int32, s.shape, 1)
            s = jnp.where(col <= row, s, NEG_INF)
        m_prev = m_ref[...]
        m_new = jnp.maximum(m_prev, jnp.max(s, axis=-1, keepdims=True))
        alpha = jnp.exp(m_prev - m_new)
        p = jnp.exp(s - m_new)
        l_ref[...] = alpha * l_ref[...] + jnp.sum(p, axis=-1, keepdims=True)
        acc_ref[...] = alpha * acc_ref[...] + _dot(p.astype(BF16), v)
        m_ref[...] = m_new

    def body(kc, carry):
        step(kc, False)
        return carry

    lax.fori_loop(0, qi, body, 0)
    step(qi, True)
    o_ref[...] = (acc_ref[...] / l_ref[...]).astype(o_ref.dtype)


def _mla_prompt_attention(q, kv, batch, seq, tq):
    nq = seq // tq
    v_block0 = MLA_HEADS * HEAD_PAD // V_DIM
    return pl.pallas_call(
        functools.partial(_flash_kernel, tq=tq),
        grid=(batch, MLA_HEADS, nq),
        in_specs=[
            pl.BlockSpec((tq, HEAD_PAD), lambda b, h, i: (b * nq + i, h)),
            pl.BlockSpec((seq, HEAD_PAD), lambda b, h, i: (b, h)),
            pl.BlockSpec((seq, V_DIM), lambda b, h, i: (b, v_block0 + h)),
        ],
        out_specs=pl.BlockSpec((tq, V_DIM), lambda b, h, i: (b * nq + i, h)),
        out_shape=jax.ShapeDtypeStruct((batch * seq, MLA_HEADS * V_DIM), BF16),
        scratch_shapes=[
            pltpu.VMEM((tq, 1), F32),
            pltpu.VMEM((tq, 1), F32),
            pltpu.VMEM((tq, V_DIM), F32),
        ],
        compiler_params=_cparams("parallel", "parallel", "arbitrary"),
        name="mla_prompt_attention",
    )(q, kv, kv)


def _head_mm_kernel(a_ref, w_ref, o_ref):
    o_ref[...] = _dot(a_ref[...], w_ref[...]).astype(o_ref.dtype)


def _q_absorb(q_s, w_uk_t):
    db = q_s.shape[0]
    return pl.pallas_call(
        _head_mm_kernel,
        grid=(MLA_HEADS,),
        in_specs=[
            pl.BlockSpec((db, NOPE_DIM), lambda h: (0, 2 * h)),
            pl.BlockSpec((None, NOPE_DIM, KV_LORA), lambda h: (h, 0, 0)),
        ],
        out_specs=pl.BlockSpec((None, db, KV_LORA), lambda h: (h, 0, 0)),
        out_shape=jax.ShapeDtypeStruct((MLA_HEADS, db, KV_LORA), BF16),
        compiler_params=_cparams("parallel"),
        name="mla_q_absorb",
    )(q_s, w_uk_t)


def _v_up(o_lat_t, w_uv_t):
    db = o_lat_t.shape[1]
    return pl.pallas_call(
        _head_mm_kernel,
        grid=(MLA_HEADS,),
        in_specs=[
            pl.BlockSpec((None, db, KV_LORA), lambda h: (h, 0, 0)),
            pl.BlockSpec((None, KV_LORA, V_DIM), lambda h: (h, 0, 0)),
        ],
        out_specs=pl.BlockSpec((db, V_DIM), lambda h: (0, h)),
        out_shape=jax.ShapeDtypeStruct((db, MLA_HEADS * V_DIM), BF16),
        compiler_params=_cparams("parallel"),
        name="mla_v_up",
    )(o_lat_t, w_uv_t)


def _mla_decode_kernel(pt_ref, q_ref, new_ref, *rest, n_pages, n_chunks):
    pages = rest[:n_pages]
    o_ref = rest[n_pages]
    kv_ref, m_ref, l_ref, acc_ref = rest[n_pages + 1:]
    c = pl.program_id(1)

    @pl.when(c == 0)
    def _():
        m_ref[...] = jnp.full(m_ref.shape, -jnp.inf, F32)
        l_ref[...] = jnp.zeros(l_ref.shape, F32)
        acc_ref[...] = jnp.zeros(acc_ref.shape, F32)

    for p in range(n_pages):
        kv_ref[p * PAGE_SIZE:(p + 1) * PAGE_SIZE, :] = pages[p][...].astype(BF16)

    q = q_ref[...]
    kv = kv_ref[...]
    s = _dot_nt(q, kv) * MLA_SCALE
    m_prev = m_ref[...]
    m_new = jnp.maximum(m_prev, jnp.max(s, axis=-1, keepdims=True))
    alpha = jnp.exp(m_prev - m_new)
    p = jnp.exp(s - m_new)
    l_ref[...] = alpha * l_ref[...] + jnp.sum(p, axis=-1, keepdims=True)
    acc_ref[...] = alpha * acc_ref[...] + _dot(p.astype(BF16), kv)
    m_ref[...] = m_new

    @pl.when(c == n_chunks - 1)
    def _():
        new = new_ref[...].astype(BF16).astype(F32)
        s_new = jnp.sum(q.astype(F32) * new, axis=-1, keepdims=True) * MLA_SCALE
        m_old = m_ref[...]
        m_fin = jnp.maximum(m_old, s_new)
        a_fin = jnp.exp(m_old - m_fin)
        p_new = jnp.exp(s_new - m_fin)
        l_fin = a_fin * l_ref[...] + p_new
        acc = a_fin * acc_ref[...] + p_new.astype(BF16).astype(F32) * new
        o_ref[...] = (acc / l_fin)[:, :KV_LORA]


def _mla_decode(q_cat, rows_new, cache, page_table):
    db = q_cat.shape[0]
    n_tab = page_table.shape[1]
    n_pages = min(DEC_PAGES, n_tab)
    n_chunks = n_tab // n_pages

    def page_spec(p):
        return pl.BlockSpec((None, PAGE_SIZE, ROW_DIM),
                            lambda b, c, pt: (pt[b * n_tab + c * n_pages + p], 0, 0))

    grid_spec = pltpu.PrefetchScalarGridSpec(
        num_scalar_prefetch=1,
        grid=(db, n_chunks),
        in_specs=[
            pl.BlockSpec((None, MLA_HEADS, ROW_DIM), lambda b, c, pt: (b, 0, 0)),
            pl.BlockSpec((None, 1, ROW_DIM), lambda b, c, pt: (b, 0, 0)),
        ] + [page_spec(p) for p in range(n_pages)],
        out_specs=pl.BlockSpec((None, MLA_HEADS, KV_LORA), lambda b, c, pt: (b, 0, 0)),
        scratch_shapes=[
            pltpu.VMEM((n_pages * PAGE_SIZE, ROW_DIM), BF16),
            pltpu.VMEM((MLA_HEADS, 1), F32),
            pltpu.VMEM((MLA_HEADS, 1), F32),
            pltpu.VMEM((MLA_HEADS, ROW_DIM), F32),
        ],
    )
    return pl.pallas_call(
        functools.partial(_mla_decode_kernel, n_pages=n_pages, n_chunks=n_chunks),
        grid_spec=grid_spec,
        out_shape=jax.ShapeDtypeStruct((db, MLA_HEADS, KV_LORA), F32),
        compiler_params=_cparams("parallel", "arbitrary"),
        name="mla_decode",
    )(page_table.reshape(-1), q_cat, rows_new.reshape(db, 1, ROW_DIM), *([cache] * n_pages))


def _dil_prompt_kernel(q_ref, kc_ref, kp_ref, vc_ref, vp_ref, bias_ref, o_ref, lse_ref):
    first = pl.program_id(2) == 0
    for h in range(GROUP_HEADS):
        hs = slice(h * B_HEAD_DIM, (h + 1) * B_HEAD_DIM)
        q = q_ref[:, hs]
        k = jnp.concatenate([kp_ref[:, hs], kc_ref[:, hs]], axis=0).astype(BF16)
        v = jnp.concatenate([vp_ref[:, hs], vc_ref[:, hs]], axis=0).astype(BF16)
        s = _dot_nt(q, k) * B_SCALE + bias_ref[h]
        col = lax.broadcasted_iota(jnp.int32, s.shape, 1)
        s = jnp.where(jnp.logical_and(first, col < DIL_TILE), NEG_INF, s)
        m = jnp.max(s, axis=-1, keepdims=True)
        p = jnp.exp(s - m)
        l = jnp.sum(p, axis=-1, keepdims=True)
        o_ref[:, hs] = _dot(p.astype(BF16), v) / l
        lse_ref[:, hs] = jnp.broadcast_to(m + jnp.log(l), (DIL_TILE, B_HEAD_DIM))


def _dil_prompt(q, kv, bias, g, dil, batch, seq):
    sd = seq // dil
    nq = sd // DIL_TILE
    q3 = q.reshape(batch, sd, dil * N_GROUPS * GROUP_DIM)
    kv3 = kv.reshape(batch, sd, dil * 2 * N_GROUPS * GROUP_DIM)
    blk = (None, DIL_TILE, GROUP_DIM)
    kcol = lambda r: r * 2 * N_GROUPS + g
    vcol = lambda r: r * 2 * N_GROUPS + N_GROUPS + g
    prev = lambda i: jnp.maximum(i - 1, 0)
    o, lse = pl.pallas_call(
        _dil_prompt_kernel,
        grid=(batch, dil, nq),
        in_specs=[
            pl.BlockSpec(blk, lambda b, r, i: (b, i, r * N_GROUPS + g)),
            pl.BlockSpec(blk, lambda b, r, i: (b, i, kcol(r))),
            pl.BlockSpec(blk, lambda b, r, i: (b, prev(i), kcol(r))),
            pl.BlockSpec(blk, lambda b, r, i: (b, i, vcol(r))),
            pl.BlockSpec(blk, lambda b, r, i: (b, prev(i), vcol(r))),
            pl.BlockSpec((GROUP_HEADS, DIL_TILE, 2 * DIL_TILE), lambda b, r, i: (0, 0, 0)),
        ],
        out_specs=[
            pl.BlockSpec(blk, lambda b, r, i: (b, i, r)),
            pl.BlockSpec(blk, lambda b, r, i: (b, i, r)),
        ],
        out_shape=[
            jax.ShapeDtypeStruct((batch, sd, dil * GROUP_DIM), F32),
            jax.ShapeDtypeStruct((batch, sd, dil * GROUP_DIM), F32),
        ],
        compiler_params=_cparams("parallel", "parallel", "arbitrary"),
        name="dilated_prompt",
    )(q3, kv3, kv3, kv3, kv3, bias)
    return o.reshape(batch * seq, GROUP_DIM), lse.reshape(batch * seq, GROUP_DIM)


def _dil_sample_kernel(q_ref, kn_ref, vn_ref, buf_ref, bias_ref, o_ref, lse_ref):
    shape = (8, GROUP_DIM)
    head_of_lane = lax.broadcasted_iota(jnp.int32, shape, 1) // B_HEAD_DIM
    own = lax.broadcasted_iota(jnp.int32, shape, 0) == head_of_lane
    bias_keys = bias_ref[:, :DIL_KEYS]
    bias_new = bias_ref[:, DIL_KEYS:DIL_KEYS + 1]
    for b in range(DIL_SAMPLE_BLOCK):
        qh = jnp.where(own, q_ref[b:b + 1, :], 0.0).astype(BF16)
        k = buf_ref[b, :, :GROUP_DIM].astype(BF16)
        v = buf_ref[b, :, GROUP_DIM:].astype(BF16)
        kn = kn_ref[b:b + 1, :].astype(BF16).astype(F32)
        vn = vn_ref[b:b + 1, :].astype(BF16).astype(F32)
        s = _dot_nt(qh, k) * B_SCALE + bias_keys
        s_new = jnp.sum(qh.astype(F32) * kn, axis=-1, keepdims=True) * B_SCALE + bias_new
        m = jnp.maximum(jnp.max(s, axis=-1, keepdims=True), s_new)
        p = jnp.exp(s - m)
        p_new = jnp.exp(s_new - m)
        l = jnp.sum(p, axis=-1, keepdims=True) + p_new
        o = (_dot(p.astype(BF16), v) + p_new.astype(BF16).astype(F32) * vn) / l
        o_ref[b:b + 1, :] = jnp.sum(jnp.where(own, o, 0.0), axis=0, keepdims=True)
        lse_ref[b:b + 1, :] = jnp.sum(jnp.where(own, m + jnp.log(l), 0.0), axis=0, keepdims=True)


def _dil_sample(q_s, kv_s, buf, bias, g, dil):
    db, length = buf.shape[:2]
    bb = DIL_SAMPLE_BLOCK
    buf3 = buf.reshape(db, length // dil, dil * 2 * GROUP_DIM)
    return pl.pallas_call(
        _dil_sample_kernel,
        grid=(db // bb,),
        in_specs=[
            pl.BlockSpec((bb, GROUP_DIM), lambda i: (i, g)),
            pl.BlockSpec((bb, GROUP_DIM), lambda i: (i, g)),
            pl.BlockSpec((bb, GROUP_DIM), lambda i: (i, N_GROUPS + g)),
            pl.BlockSpec((bb, DIL_KEYS, 2 * GROUP_DIM), lambda i: (i, 0, 0)),
            pl.BlockSpec((8, 2 * DIL_KEYS), lambda i: (0, 0)),
        ],
        out_specs=[
            pl.BlockSpec((bb, GROUP_DIM), lambda i: (i, 0)),
            pl.BlockSpec((bb, GROUP_DIM), lambda i: (i, 0)),
        ],
        out_shape=[
            jax.ShapeDtypeStruct((db, GROUP_DIM), F32),
            jax.ShapeDtypeStruct((db, GROUP_DIM), F32),
        ],
        compiler_params=_cparams("parallel"),
        name="dilated_sample",
    )(q_s, kv_s, kv_s, buf3, bias)


def _merge_out_kernel(o0, o1, o2, l0, l1, l2, w_ref, x_ref, g_ref, out_ref):
    a0, a1, a2 = l0[...], l1[...], l2[...]
    mx = jnp.maximum(jnp.maximum(a0, a1), a2)
    e0, e1, e2 = jnp.exp(a0 - mx), jnp.exp(a1 - mx), jnp.exp(a2 - mx)
    merged = (e0 * o0[...] + e1 * o1[...] + e2 * o2[...]) / (e0 + e1 + e2)
    y = _dot(merged.astype(BF16), w_ref[...])
    out_ref[...] = x_ref[...] + _rms(y, g_ref[...])


def _merge_out(outs, lses, w, x, g, tm):
    m, d = x.shape
    part = pl.BlockSpec((tm, GROUP_DIM), lambda i: (i, 0))
    return pl.pallas_call(
        _merge_out_kernel,
        grid=(m // tm,),
        in_specs=[part] * 6 + [
            pl.BlockSpec((GROUP_DIM, d), lambda i: (0, 0)),
            pl.BlockSpec((tm, d), lambda i: (i, 0)),
            pl.BlockSpec((1, d), lambda i: (0, 0)),
        ],
        out_specs=pl.BlockSpec((tm, d), lambda i: (i, 0)),
        out_shape=jax.ShapeDtypeStruct((m, d), F32),
        compiler_params=_cparams("parallel"),
        name="dilated_merge_out",
    )(*outs, *lses, w, x, g)


def _t5_bucket(dist):
    max_exact = T5_BUCKETS // 2
    d = np.maximum(dist, 1).astype(np.float32)
    large = max_exact + (np.log(d / max_exact) / math.log(T5_MAX_DIST / max_exact)
                         * (T5_BUCKETS - max_exact)).astype(np.int32)
    large = np.minimum(large, T5_BUCKETS - 1)
    return np.where(dist < max_exact, dist, large).astype(np.int32)


def _group_bias(t5_bias, g):
    window, dil = DIL_GROUPS[g]
    buckets = _t5_bucket(dil * np.arange(window // dil + 1))
    return t5_bias[buckets, g * GROUP_HEADS:(g + 1) * GROUP_HEADS].astype(F32)


def _prompt_bias(bias):
    j = np.arange(DIL_TILE)[:, None] + DIL_TILE - np.arange(2 * DIL_TILE)[None, :]
    valid = (j >= 0) & (j <= DIL_KEYS)
    band = bias[np.clip(j, 0, DIL_KEYS)]
    band = jnp.where(valid[:, :, None], band, NEG_INF)
    return jnp.transpose(band, (2, 0, 1))


def _sample_bias(bias):
    keys = bias[DIL_KEYS - np.arange(DIL_KEYS)].T
    own = jnp.broadcast_to(bias[0][:, None], (GROUP_HEADS, DIL_KEYS))
    both = jnp.concatenate([keys, own], axis=1)
    return jnp.pad(both, ((0, 8 - GROUP_HEADS), (0, 0)))


def _rope_tables(pos):
    half = ROPE_DIM // 2
    inv = ROPE_THETA ** (-jnp.arange(half, dtype=F32) / half)
    ang = pos.astype(F32)[:, None] * inv
    cos, sin = jnp.cos(ang), jnp.sin(ang)
    zero = jnp.zeros((pos.shape[0], LANES - ROPE_DIM), F32)
    return (jnp.concatenate([cos, cos, zero], axis=1), jnp.concatenate([-sin, sin, zero], axis=1))


def _tile(n, pref):
    return pref if n % pref == 0 else n


def kernel(x_prompt, x_sample, cache_mla, page_table, cache_b_win0, cache_b_win1, cache_b_win2,
           norm_gains, w_mla_in, g_q_a, g_kv_a, w_uq, w_uk, w_uv, w_mla_o,
           g_kv_b, w_kv_b, w_q_b, w_b_o, t5_bias, w_ffn_in, w_ffn_out):
    batch, seq, d = x_prompt.shape
    db = x_sample.shape[0]
    assert x_sample.shape[1] == 1 and DEPTH == 2 and N_A_LAYERS == 1
    win_bufs = (cache_b_win0, cache_b_win1, cache_b_win2)
    for gi, (window, dil) in enumerate(DIL_GROUPS):
        assert win_bufs[gi].shape[1] == window and window // dil == DIL_KEYS
        assert seq % (dil * DIL_TILE) == 0
    mp = batch * seq
    xp = x_prompt.reshape(mp, d)
    xs = x_sample.reshape(db, d)
    gains = norm_gains.reshape(DEPTH, 4, 1, d)
    tm_p = _tile(mp, 1024)

    rope_cols = jnp.pad(w_mla_in[0][:, Q_LORA + KV_LORA:], ((0, 0), (0, LANES - ROPE_DIM)))
    w_in_pad = jnp.concatenate([w_mla_in[0][:, :Q_LORA + KV_LORA], rope_cols], axis=1).astype(BF16)
    w_uq_pad = jnp.pad(w_uq[0].reshape(Q_LORA, MLA_HEADS, NOPE_DIM + ROPE_DIM),
                       ((0, 0), (0, 0), (0, HEAD_PAD - NOPE_DIM - ROPE_DIM)))
    w_uq_pad = w_uq_pad.reshape(Q_LORA, MLA_HEADS * HEAD_PAD).astype(BF16)
    w_k_lat = jnp.pad(w_uk[0], ((0, 0), (0, 0), (0, HEAD_PAD - NOPE_DIM)))
    rope_copy = jnp.zeros((ROPE_DIM, MLA_HEADS, HEAD_PAD), F32)
    rope_copy = rope_copy.at[np.arange(ROPE_DIM), :, NOPE_DIM + np.arange(ROPE_DIM)].set(1.0)
    w_k = jnp.concatenate([w_k_lat, rope_copy], axis=0).reshape(ROW_DIM, MLA_HEADS * HEAD_PAD)
    w_v = jnp.pad(w_uv[0].reshape(KV_LORA, MLA_HEADS * V_DIM), ((0, ROPE_DIM), (0, 0)))
    w_kv = jnp.concatenate([w_k, w_v], axis=1).astype(BF16)
    w_uk_t = jnp.transpose(w_uk[0], (1, 2, 0)).astype(BF16)
    w_uv_t = jnp.transpose(w_uv[0], (1, 0, 2)).astype(BF16)
    w_o = w_mla_o[0].astype(BF16)
    w_kvb = w_kv_b.astype(BF16)
    w_qb = w_q_b[0].astype(BF16)
    w_bo = w_b_o[0].astype(BF16)
    w_f_in = w_ffn_in.astype(BF16)
    w_f_out = w_ffn_out.astype(BF16)
    f = w_ffn_out.shape[1]
    tf = _tile(f, 512)

    c_p, s_p = _rope_tables(jnp.tile(jnp.arange(seq), batch))
    c_s, s_s = _rope_tables(jnp.full((db,), PAST_LEN, jnp.int32))

    g = gains[0]
    cq_p, rows_p = _mla_in(xp, g[0], w_in_pad, g_q_a[0:1], g_kv_a[0:1], c_p, s_p, _tile(mp, 512))
    cq_s, rows_s = _mla_in(xs, g[0], w_in_pad, g_q_a[0:1], g_kv_a[0:1], c_s, s_s, db)
    q_p = _q_up(cq_p, w_uq_pad, c_p, s_p, tm_p, 1024)
    q_s = _q_up(cq_s, w_uq_pad, c_s, s_s, db, 1024)

    kv_p = _cast_mm(rows_p, w_kv, BF16, tm_p, 1024)
    attn_p = _mla_prompt_attention(q_p, kv_p, batch, seq, _tile(seq, 512))
    xp = _mm_norm_res(attn_p, w_o, xp, g[1], tm_p, 512)

    q_lat = _q_absorb(q_s, w_uk_t)
    q_rope_s = q_s.reshape(db, MLA_HEADS, HEAD_PAD)[:, :, NOPE_DIM:NOPE_DIM + ROPE_DIM]
    q_cat = jnp.concatenate([jnp.transpose(q_lat, (1, 0, 2)), q_rope_s], axis=-1)
    o_lat = _mla_decode(q_cat, rows_s, cache_mla[0], page_table)
    attn_s = _v_up(jnp.transpose(o_lat, (1, 0, 2)).astype(BF16), w_uv_t)
    xs = _mm_norm_res(attn_s, w_o, xs, g[1], db, 512)

    xp = _ffn(xp, g[2], g[3], w_f_in[0], w_f_out[0], _tile(mp, 512), tf)
    xs = _ffn(xs, g[2], g[3], w_f_in[0], w_f_out[0], db, tf)

    g = gains[1]
    g_kvb = g_kv_b.reshape(1, d)
    kvb_p = _norm_mm(xp, g_kvb, w_kvb, F32, tm_p, 512)
    kvb_s = _norm_mm(xs, g_kvb, w_kvb, F32, db, 512)
    qb_p = _norm_mm(xp, g[0], w_qb, BF16, tm_p, GROUP_DIM)
    qb_s = _norm_mm(xs, g[0], w_qb, F32, db, GROUP_DIM)
    outs_p, lses_p, outs_s, lses_s = [], [], [], []
    for gi, (window, dil) in enumerate(DIL_GROUPS):
        bias = _group_bias(t5_bias, gi)
        o, lse = _dil_prompt(qb_p, kvb_p, _prompt_bias(bias), gi, dil, batch, seq)
        outs_p.append(o)
        lses_p.append(lse)
        o, lse = _dil_sample(qb_s, kvb_s, win_bufs[gi], _sample_bias(bias), gi, dil)
        outs_s.append(o)
        lses_s.append(lse)
    xp = _merge_out(outs_p, lses_p, w_bo, xp, g[1], _tile(mp, 256))
    xs = _merge_out(outs_s, lses_s, w_bo, xs, g[1], db)

    xp = _ffn(xp, g[2], g[3], w_f_in[1], w_f_out[1], _tile(mp, 512), tf)
    xs = _ffn(xs, g[2], g[3], w_f_in[1], w_f_out[1], db, tf)

    kv6_p = kvb_p.reshape(batch, seq, 2, N_GROUPS, GROUP_HEADS, B_HEAD_DIM)
    kv6_s = kvb_s.reshape(db, 1, 2, N_GROUPS, GROUP_HEADS, B_HEAD_DIM)
    wins = []
    for gi, (window, _) in enumerate(DIL_GROUPS):
        wins.append(kv6_p[:, seq - min(window, seq):, :, gi])
        wins.append(kv6_s[:, :, :, gi])
    return (xp.reshape(batch, seq, d), xs.reshape(db, 1, d),
            rows_p.reshape(1, batch, seq, ROW_DIM), rows_s.reshape(1, db, 1, ROW_DIM), *wins)
```

```python
import functools
import math

import numpy as np
import jax
import jax.numpy as jnp
from jax import lax
from jax.experimental import pallas as pl
from jax.experimental.pallas import tpu as pltpu

D_MODEL = 2048
DEPTH = 2
PAST_LEN = 16384
PAGE_SIZE = 128
N_A_LAYERS = DEPTH // 2
MLA_HEADS = 16
Q_LORA = 512
KV_LORA = 512
NOPE_DIM = 128
ROPE_DIM = 64
V_DIM = 128
ROPE_THETA = 10000.0
MLA_SCALE = (NOPE_DIM + ROPE_DIM) ** -0.5
DIL_GROUPS = ((128, 1), (512, 4), (2048, 16))
N_GROUPS = len(DIL_GROUPS)
GROUP_HEADS = 6
B_HEAD_DIM = 128
B_HEADS = N_GROUPS * GROUP_HEADS
B_SCALE = B_HEAD_DIM ** -0.5
T5_BUCKETS = 32
T5_MAX_DIST = 2048
EPS = 1e-6
NEG_INF = -1e30

F32 = jnp.float32
BF16 = jnp.bfloat16

VMEM_LIMIT_BYTES = 56 * 1024 * 1024
LANES = 128
SUBLANES = 8
HEAD_PAD = 2 * LANES
ROW_DIM = KV_LORA + ROPE_DIM
GROUP_DIM = GROUP_HEADS * B_HEAD_DIM
QKV_DIM = 3 * GROUP_DIM
DIL_KEYS = 128
DIL_TILE = 128
DEC_PAGES = 16
DIL_SAMPLE_BLOCK = 8
BUF_ROWS = 2 * GROUP_HEADS
BUF_ROWS_PAD = 16


def _cparams(*semantics):
    return pltpu.CompilerParams(dimension_semantics=semantics, vmem_limit_bytes=VMEM_LIMIT_BYTES)


def _rms(x, g):
    return x * lax.rsqrt(jnp.mean(x * x, axis=-1, keepdims=True) + EPS) * g


def _dot(a, b):
    return jnp.dot(a, b, preferred_element_type=F32)


def _dot_nt(a, b):
    return lax.dot_general(a, b, (((1,), (1,)), ((), ())), preferred_element_type=F32)


def _dot_tn(a, b):
    return lax.dot_general(a, b, (((0,), (0,)), ((), ())), preferred_element_type=F32)


def _rope128(x, c, s):
    lane = lax.broadcasted_iota(jnp.int32, x.shape, 1)
    half = ROPE_DIM // 2
    swapped = jnp.where(lane < half, pltpu.roll(x, LANES - half, 1), pltpu.roll(x, half, 1))
    return x * c + swapped * s


def _mla_in_kernel(x_ref, g_ref, w_ref, gq_ref, gkv_ref, c_ref, s_ref, cq_ref, rows_ref):
    xn = _rms(x_ref[...], g_ref[...]).astype(BF16)
    a = _dot(xn, w_ref[...])
    cq_ref[...] = _rms(a[:, :Q_LORA], gq_ref[...]).astype(BF16)
    rows_ref[:, :KV_LORA] = _rms(a[:, Q_LORA:Q_LORA + KV_LORA], gkv_ref[...])
    kr = _rope128(a[:, Q_LORA + KV_LORA:], c_ref[...], s_ref[...])
    rows_ref[:, KV_LORA:] = kr[:, :ROPE_DIM]


def _mla_in(x, g, w_pad, gq, gkv, c_tab, s_tab, tm):
    m, d = x.shape
    n = w_pad.shape[1]
    return pl.pallas_call(
        _mla_in_kernel,
        grid=(m // tm,),
        in_specs=[
            pl.BlockSpec((tm, d), lambda i: (i, 0)),
            pl.BlockSpec((1, d), lambda i: (0, 0)),
            pl.BlockSpec((d, n), lambda i: (0, 0)),
            pl.BlockSpec((1, Q_LORA), lambda i: (0, 0)),
            pl.BlockSpec((1, KV_LORA), lambda i: (0, 0)),
            pl.BlockSpec((tm, LANES), lambda i: (i, 0)),
            pl.BlockSpec((tm, LANES), lambda i: (i, 0)),
        ],
        out_specs=[
            pl.BlockSpec((tm, Q_LORA), lambda i: (i, 0)),
            pl.BlockSpec((tm, ROW_DIM), lambda i: (i, 0)),
        ],
        out_shape=[
            jax.ShapeDtypeStruct((m, Q_LORA), BF16),
            jax.ShapeDtypeStruct((m, ROW_DIM), F32),
        ],
        compiler_params=_cparams("parallel"),
        name="mla_in",
    )(x, g, w_pad, gq, gkv, c_tab, s_tab)


def _q_up_kernel(cq_ref, w_ref, c_ref, s_ref, o_ref, *, heads_per_tile):
    a = _dot(cq_ref[...], w_ref[...])
    c = c_ref[...]
    s = s_ref[...]
    for h in range(heads_per_tile):
        lo = h * HEAD_PAD
        o_ref[:, lo:lo + LANES] = a[:, lo:lo + LANES].astype(BF16)
        o_ref[:, lo + LANES:lo + HEAD_PAD] = _rope128(a[:, lo + LANES:lo + HEAD_PAD], c, s).astype(BF16)


def _q_up(cq, w_pad, c_tab, s_tab, tm, tn):
    m, k = cq.shape
    n = w_pad.shape[1]
    return pl.pallas_call(
        functools.partial(_q_up_kernel, heads_per_tile=tn // HEAD_PAD),
        grid=(m // tm, n // tn),
        in_specs=[
            pl.BlockSpec((tm, k), lambda i, j: (i, 0)),
            pl.BlockSpec((k, tn), lambda i, j: (0, j)),
            pl.BlockSpec((tm, LANES), lambda i, j: (i, 0)),
            pl.BlockSpec((tm, LANES), lambda i, j: (i, 0)),
        ],
        out_specs=pl.BlockSpec((tm, tn), lambda i, j: (i, j)),
        out_shape=jax.ShapeDtypeStruct((m, n), BF16),
        compiler_params=_cparams("parallel", "parallel"),
        name="mla_q_up",
    )(cq, w_pad, c_tab, s_tab)


def _cast_mm_kernel(x_ref, w_ref, o_ref):
    o_ref[...] = _dot(x_ref[...].astype(BF16), w_ref[...]).astype(o_ref.dtype)


def _cast_mm(x, w, out_dtype, tm, tn):
    m, k = x.shape
    n = w.shape[1]
    return pl.pallas_call(
        _cast_mm_kernel,
        grid=(m // tm, n // tn),
        in_specs=[
            pl.BlockSpec((tm, k), lambda i, j: (i, 0)),
            pl.BlockSpec((k, tn), lambda i, j: (0, j)),
        ],
        out_specs=pl.BlockSpec((tm, tn), lambda i, j: (i, j)),
        out_shape=jax.ShapeDtypeStruct((m, n), out_dtype),
        compiler_params=_cparams("parallel", "parallel"),
        name="cast_mm",
    )(x, w)


def _norm_mm_kernel(x_ref, g_ref, w_ref, o_ref, xn_ref):
    @pl.when(pl.program_id(1) == 0)
    def _():
        xn_ref[...] = _rms(x_ref[...], g_ref[...]).astype(BF16)

    o_ref[...] = _dot(xn_ref[...], w_ref[...]).astype(o_ref.dtype)


def _norm_mm(x, g, w, out_dtype, tm, tn):
    m, d = x.shape
    n = w.shape[1]
    return pl.pallas_call(
        _norm_mm_kernel,
        grid=(m // tm, n // tn),
        in_specs=[
            pl.BlockSpec((tm, d), lambda i, j: (i, 0)),
            pl.BlockSpec((1, d), lambda i, j: (0, 0)),
            pl.BlockSpec((d, tn), lambda i, j: (0, j)),
        ],
        out_specs=pl.BlockSpec((tm, tn), lambda i, j: (i, j)),
        out_shape=jax.ShapeDtypeStruct((m, n), out_dtype),
        scratch_shapes=[pltpu.VMEM((tm, d), BF16)],
        compiler_params=_cparams("parallel", "arbitrary"),
        name="norm_mm",
    )(x, g, w)


def _mm_norm_res_kernel(a_ref, w_ref, x_ref, g_ref, o_ref, *, nk):
    k = pl.program_id(1)
    part = _dot(a_ref[...], w_ref[...])

    @pl.when(k == 0)
    def _():
        o_ref[...] = part

    @pl.when(k > 0)
    def _():
        o_ref[...] += part

    @pl.when(k == nk - 1)
    def _():
        o_ref[...] = x_ref[...] + _rms(o_ref[...], g_ref[...])


def _mm_norm_res(a, w, x, g, tm, tk):
    m, kdim = a.shape
    d = w.shape[1]
    nk = kdim // tk
    return pl.pallas_call(
        functools.partial(_mm_norm_res_kernel, nk=nk),
        grid=(m // tm, nk),
        in_specs=[
            pl.BlockSpec((tm, tk), lambda i, k: (i, k)),
            pl.BlockSpec((tk, d), lambda i, k: (k, 0)),
            pl.BlockSpec((tm, d), lambda i, k: (i, 0)),
            pl.BlockSpec((1, d), lambda i, k: (0, 0)),
        ],
        out_specs=pl.BlockSpec((tm, d), lambda i, k: (i, 0)),
        out_shape=jax.ShapeDtypeStruct((m, d), F32),
        compiler_params=_cparams("parallel", "arbitrary"),
        name="mm_norm_res",
    )(a, w, x, g)


def _ffn_kernel(x_ref, g_in_ref, g_out_ref, wg_ref, wu_ref, wo_ref, o_ref, xn_ref, *, nf):
    j = pl.program_id(1)

    @pl.when(j == 0)
    def _():
        xn_ref[...] = _rms(x_ref[...], g_in_ref[...]).astype(BF16)

    xn = xn_ref[...]
    gate = _dot(xn, wg_ref[...])
    up = _dot(xn, wu_ref[...])
    act = (gate * jax.nn.sigmoid(gate) * up).astype(BF16)
    part = _dot(act, wo_ref[...])

    @pl.when(j == 0)
    def _():
        o_ref[...] = part

    @pl.when(j > 0)
    def _():
        o_ref[...] += part

    @pl.when(j == nf - 1)
    def _():
        o_ref[...] = x_ref[...] + _rms(o_ref[...], g_out_ref[...])


def _ffn(x, g_in, g_out, w_in, w_out, tm, tf):
    m, d = x.shape
    f = w_out.shape[0]
    nf = f // tf
    return pl.pallas_call(
        functools.partial(_ffn_kernel, nf=nf),
        grid=(m // tm, nf),
        in_specs=[
            pl.BlockSpec((tm, d), lambda i, j: (i, 0)),
            pl.BlockSpec((1, d), lambda i, j: (0, 0)),
            pl.BlockSpec((1, d), lambda i, j: (0, 0)),
            pl.BlockSpec((d, tf), lambda i, j: (0, j)),
            pl.BlockSpec((d, tf), lambda i, j: (0, j + nf)),
            pl.BlockSpec((tf, d), lambda i, j: (j, 0)),
        ],
        out_specs=pl.BlockSpec((tm, d), lambda i, j: (i, 0)),
        out_shape=jax.ShapeDtypeStruct((m, d), F32),
        scratch_shapes=[pltpu.VMEM((tm, d), BF16)],
        compiler_params=_cparams("parallel", "arbitrary"),
        name="ffn",
    )(x, g_in, g_out, w_in, w_in, w_out)


def _flash_kernel(q_ref, k_ref, v_ref, o_ref, m_ref, l_ref, acc_ref, *, tq):
    qi = pl.program_id(2)
    qt = q_ref[...].astype(F32).T.astype(BF16)
    m_ref[...] = jnp.full(m_ref.shape, -jnp.inf, F32)
    l_ref[...] = jnp.zeros(l_ref.shape, F32)
    acc_ref[...] = jnp.zeros(acc_ref.shape, F32)

    def step(kc, diagonal):
        start = pl.multiple_of(kc * tq, tq)
        k = k_ref[pl.ds(start, tq), :]
        v = v_ref[pl.ds(start, tq), :]
        s = _dot(k, qt) * MLA_SCALE
        if diagonal:
            key = lax.broadcasted_iota(jnp.int32, s.shape, 0)
            query = lax.broadcasted_iota(jnp.int32, s.shape, 1)
            s = jnp.where(key <= query, s, NEG_INF)
        m_prev = m_ref[...]
        m_new = jnp.maximum(m_prev, jnp.max(s, axis=0, keepdims=True))
        alpha = jnp.exp(m_prev - m_new)
        p = jnp.exp(s - m_new)
        l_ref[...] = alpha * l_ref[...] + jnp.sum(p, axis=0, keepdims=True)
        acc_ref[...] = alpha * acc_ref[...] + _dot_tn(v, p.astype(BF16))
        m_ref[...] = m_new

    def body(kc, carry):
        step(kc, False)
        return carry

    lax.fori_loop(0, qi, body, 0)
    step(qi, True)
    o_ref[...] = (acc_ref[...] / l_ref[...]).T.astype(o_ref.dtype)


def _mla_prompt_attention(q, kv, batch, seq, tq):
    nq = seq // tq
    v_block0 = MLA_HEADS * HEAD_PAD // V_DIM
    return pl.pallas_call(
        functools.partial(_flash_kernel, tq=tq),
        grid=(batch, MLA_HEADS, nq),
        in_specs=[
            pl.BlockSpec((tq, HEAD_PAD), lambda b, h, i: (b * nq + i, h)),
            pl.BlockSpec((seq, HEAD_PAD), lambda b, h, i: (b, h)),
            pl.BlockSpec((seq, V_DIM), lambda b, h, i: (b, v_block0 + h)),
        ],
        out_specs=pl.BlockSpec((tq, V_DIM), lambda b, h, i: (b * nq + i, h)),
        out_shape=jax.ShapeDtypeStruct((batch * seq, MLA_HEADS * V_DIM), BF16),
        scratch_shapes=[
            pltpu.VMEM((1, tq), F32),
            pltpu.VMEM((1, tq), F32),
            pltpu.VMEM((V_DIM, tq), F32),
        ],
        compiler_params=_cparams("parallel", "parallel", "arbitrary"),
        name="mla_prompt_attention",
    )(q, kv, kv)


def _head_mm_kernel(a_ref, w_ref, o_ref):
    o_ref[...] = _dot(a_ref[...], w_ref[...]).astype(o_ref.dtype)


def _q_absorb(q_s, w_uk_t):
    db = q_s.shape[0]
    return pl.pallas_call(
        _head_mm_kernel,
        grid=(MLA_HEADS,),
        in_specs=[
            pl.BlockSpec((db, NOPE_DIM), lambda h: (0, 2 * h)),
            pl.BlockSpec((None, NOPE_DIM, KV_LORA), lambda h: (h, 0, 0)),
        ],
        out_specs=pl.BlockSpec((None, db, KV_LORA), lambda h: (h, 0, 0)),
        out_shape=jax.ShapeDtypeStruct((MLA_HEADS, db, KV_LORA), BF16),
        compiler_params=_cparams("parallel"),
        name="mla_q_absorb",
    )(q_s, w_uk_t)


def _v_up(o_lat_t, w_uv_t):
    db = o_lat_t.shape[1]
    return pl.pallas_call(
        _head_mm_kernel,
        grid=(MLA_HEADS,),
        in_specs=[
            pl.BlockSpec((None, db, KV_LORA), lambda h: (h, 0, 0)),
            pl.BlockSpec((None, KV_LORA, V_DIM), lambda h: (h, 0, 0)),
        ],
        out_specs=pl.BlockSpec((db, V_DIM), lambda h: (0, h)),
        out_shape=jax.ShapeDtypeStruct((db, MLA_HEADS * V_DIM), BF16),
        compiler_params=_cparams("parallel"),
        name="mla_v_up",
    )(o_lat_t, w_uv_t)


def _mla_decode_kernel(pt_ref, q_ref, new_ref, *rest, n_pages, n_chunks):
    pages = rest[:n_pages]
    o_ref = rest[n_pages]
    kvt_ref, m_ref, l_ref, acc_ref = rest[n_pages + 1:]
    c = pl.program_id(1)

    @pl.when(c == 0)
    def _():
        m_ref[...] = jnp.full(m_ref.shape, -jnp.inf, F32)
        l_ref[...] = jnp.zeros(l_ref.shape, F32)
        acc_ref[...] = jnp.zeros(acc_ref.shape, F32)

    for p in range(n_pages):
        kvt_ref[:, p * PAGE_SIZE:(p + 1) * PAGE_SIZE] = pages[p][...].astype(BF16)

    q = q_ref[...]
    kvt = kvt_ref[...]
    s = _dot(q, kvt) * MLA_SCALE
    m_prev = m_ref[...]
    m_new = jnp.maximum(m_prev, jnp.max(s, axis=-1, keepdims=True))
    alpha = jnp.exp(m_prev - m_new)
    p = jnp.exp(s - m_new)
    l_ref[...] = alpha * l_ref[...] + jnp.sum(p, axis=-1, keepdims=True)
    acc_ref[...] = alpha * acc_ref[...] + _dot_nt(p.astype(BF16), kvt)
    m_ref[...] = m_new

    @pl.when(c == n_chunks - 1)
    def _():
        new = new_ref[...].astype(BF16).astype(F32)
        s_new = jnp.sum(q.astype(F32) * new, axis=-1, keepdims=True) * MLA_SCALE
        m_old = m_ref[...]
        m_fin = jnp.maximum(m_old, s_new)
        a_fin = jnp.exp(m_old - m_fin)
        p_new = jnp.exp(s_new - m_fin)
        l_fin = a_fin * l_ref[...] + p_new
        acc = a_fin * acc_ref[...] + p_new.astype(BF16).astype(F32) * new
        o_ref[...] = (acc / l_fin)[:, :KV_LORA]


def _mla_decode(q_cat, rows_new, cache_t, page_table):
    db = q_cat.shape[0]
    n_tab = page_table.shape[1]
    n_pages = min(DEC_PAGES, n_tab)
    n_chunks = n_tab // n_pages

    def page_spec(p):
        return pl.BlockSpec((None, ROW_DIM, PAGE_SIZE),
                            lambda b, c, pt: (pt[b * n_tab + c * n_pages + p], 0, 0))

    grid_spec = pltpu.PrefetchScalarGridSpec(
        num_scalar_prefetch=1,
        grid=(db, n_chunks),
        in_specs=[
            pl.BlockSpec((None, MLA_HEADS, ROW_DIM), lambda b, c, pt: (b, 0, 0)),
            pl.BlockSpec((None, 1, ROW_DIM), lambda b, c, pt: (b, 0, 0)),
        ] + [page_spec(p) for p in range(n_pages)],
        out_specs=pl.BlockSpec((None, MLA_HEADS, KV_LORA), lambda b, c, pt: (b, 0, 0)),
        scratch_shapes=[
            pltpu.VMEM((ROW_DIM, n_pages * PAGE_SIZE), BF16),
            pltpu.VMEM((MLA_HEADS, 1), F32),
            pltpu.VMEM((MLA_HEADS, 1), F32),
            pltpu.VMEM((MLA_HEADS, ROW_DIM), F32),
        ],
    )
    return pl.pallas_call(
        functools.partial(_mla_decode_kernel, n_pages=n_pages, n_chunks=n_chunks),
        grid_spec=grid_spec,
        out_shape=jax.ShapeDtypeStruct((db, MLA_HEADS, KV_LORA), F32),
        compiler_params=_cparams("parallel", "arbitrary"),
        name="mla_decode",
    )(page_table.reshape(-1), q_cat, rows_new.reshape(db, 1, ROW_DIM), *([cache_t] * n_pages))


def _dil_proj_kernel(x_ref, gq_ref, gkv_ref, w_ref, qkv_ref, kv_ref, y_ref, *, dil):
    x = x_ref[...]
    xr = x * lax.rsqrt(jnp.mean(x * x, axis=-1, keepdims=True) + EPS)
    xq = (xr * gq_ref[...]).astype(BF16)
    xkv = (xr * gkv_ref[...]).astype(BF16)
    n = x.shape[0] // dil
    for j in range(3):
        y = _dot(xq if j == 0 else xkv, w_ref[:, j * GROUP_DIM:(j + 1) * GROUP_DIM])
        if j > 0:
            kv_ref[:, (j - 1) * GROUP_DIM:j * GROUP_DIM] = y
        if dil == 1:
            qkv_ref[:, j * GROUP_DIM:(j + 1) * GROUP_DIM] = y.astype(BF16)
            continue
        for h in range(GROUP_HEADS):
            y_ref[h] = y[:, h * B_HEAD_DIM:(h + 1) * B_HEAD_DIM]
        for r in range(dil):
            for h in range(GROUP_HEADS):
                lo = r * QKV_DIM + j * GROUP_DIM + h * B_HEAD_DIM
                qkv_ref[:, lo:lo + B_HEAD_DIM] = y_ref[h, pl.ds(r, n, stride=dil), :].astype(BF16)


def _dil_proj(x, gq, gkv, w, dil, tm):
    m, d = x.shape
    return pl.pallas_call(
        functools.partial(_dil_proj_kernel, dil=dil),
        grid=(m // tm,),
        in_specs=[
            pl.BlockSpec((tm, d), lambda i: (i, 0)),
            pl.BlockSpec((1, d), lambda i: (0, 0)),
            pl.BlockSpec((1, d), lambda i: (0, 0)),
            pl.BlockSpec((d, QKV_DIM), lambda i: (0, 0)),
        ],
        out_specs=[
            pl.BlockSpec((tm // dil, dil * QKV_DIM), lambda i: (i, 0)),
            pl.BlockSpec((tm, 2 * GROUP_DIM), lambda i: (i, 0)),
        ],
        out_shape=[
            jax.ShapeDtypeStruct((m // dil, dil * QKV_DIM), BF16),
            jax.ShapeDtypeStruct((m, 2 * GROUP_DIM), F32),
        ],
        scratch_shapes=[pltpu.VMEM((GROUP_HEADS, tm, B_HEAD_DIM), F32)],
        compiler_params=_cparams("parallel"),
        name="dilated_proj",
    )(x, gq, gkv, w)


def _dil_prompt_kernel(q_ref, kc_ref, kp_ref, vc_ref, vp_ref, bias_ref, o_ref, lse_ref):
    first = pl.program_id(2) == 0
    for h in range(GROUP_HEADS):
        hs = slice(h * B_HEAD_DIM, (h + 1) * B_HEAD_DIM)
        q = q_ref[:, hs]
        k = jnp.concatenate([kp_ref[:, hs], kc_ref[:, hs]], axis=0)
        v = jnp.concatenate([vp_ref[:, hs], vc_ref[:, hs]], axis=0)
        s = _dot_nt(q, k) * B_SCALE + bias_ref[h]
        col = lax.broadcasted_iota(jnp.int32, s.shape, 1)
        s = jnp.where(jnp.logical_and(first, col < DIL_TILE), NEG_INF, s)
        m = jnp.max(s, axis=-1, keepdims=True)
        p = jnp.exp(s - m)
        l = jnp.sum(p, axis=-1, keepdims=True)
        o_ref[:, hs] = _dot(p.astype(BF16), v) / l
        lse_ref[:, hs] = jnp.broadcast_to(m + jnp.log(l), (DIL_TILE, B_HEAD_DIM))


def _dil_prompt(qkv, bias, dil, batch, seq):
    sd = seq // dil
    nq = sd // DIL_TILE
    blk = (DIL_TILE, GROUP_DIM)
    cur = lambda b, i: b * nq + i
    prev = lambda b, i: b * nq + jnp.maximum(i - 1, 0)
    return pl.pallas_call(
        _dil_prompt_kernel,
        grid=(batch, dil, nq),
        in_specs=[
            pl.BlockSpec(blk, lambda b, r, i: (cur(b, i), 3 * r)),
            pl.BlockSpec(blk, lambda b, r, i: (cur(b, i), 3 * r + 1)),
            pl.BlockSpec(blk, lambda b, r, i: (prev(b, i), 3 * r + 1)),
            pl.BlockSpec(blk, lambda b, r, i: (cur(b, i), 3 * r + 2)),
            pl.BlockSpec(blk, lambda b, r, i: (prev(b, i), 3 * r + 2)),
            pl.BlockSpec((GROUP_HEADS, DIL_TILE, 2 * DIL_TILE), lambda b, r, i: (0, 0, 0)),
        ],
        out_specs=[
            pl.BlockSpec(blk, lambda b, r, i: (cur(b, i), r)),
            pl.BlockSpec(blk, lambda b, r, i: (cur(b, i), r)),
        ],
        out_shape=[
            jax.ShapeDtypeStruct((batch * sd, dil * GROUP_DIM), F32),
            jax.ShapeDtypeStruct((batch * sd, dil * GROUP_DIM), F32),
        ],
        compiler_params=_cparams("parallel", "parallel", "arbitrary"),
        name="dilated_prompt",
    )(qkv, qkv, qkv, qkv, qkv, bias)


def _dil_sample_kernel(q_ref, kn_ref, vn_ref, buf_ref, bias_ref, o_ref, lse_ref, *, dil):
    row = lax.broadcasted_iota(jnp.int32, (SUBLANES, B_HEAD_DIM), 0)
    row_wide = lax.broadcasted_iota(jnp.int32, (SUBLANES, GROUP_DIM), 0)
    own = row_wide == lax.broadcasted_iota(jnp.int32, (SUBLANES, GROUP_DIM), 1) // B_HEAD_DIM
    bias_keys = bias_ref[:, :DIL_KEYS]
    bias_new = bias_ref[:, DIL_KEYS:DIL_KEYS + 1]

    def cached(b, r):
        if dil == 1:
            return buf_ref[b, pl.ds(r, DIL_KEYS, stride=BUF_ROWS), :]
        return buf_ref[b, :, r, :]

    for b in range(DIL_SAMPLE_BLOCK):
        qb = q_ref[b:b + 1, :]
        s = jnp.zeros((SUBLANES, DIL_KEYS), F32)
        for h in range(GROUP_HEADS):
            qh = jnp.broadcast_to(qb[:, h * B_HEAD_DIM:(h + 1) * B_HEAD_DIM], (SUBLANES, B_HEAD_DIM))
            sh = _dot_nt(qh.astype(BF16), cached(b, 2 * h).astype(BF16))
            s = jnp.where(row == h, sh, s)
        s = s * B_SCALE + bias_keys
        kn = kn_ref[b:b + 1, :].astype(BF16).astype(F32)
        vn = vn_ref[b:b + 1, :].astype(BF16).astype(F32)
        q_own = jnp.where(own, qb, 0.0).astype(BF16).astype(F32)
        s_new = jnp.sum(q_own * kn, axis=-1, keepdims=True) * B_SCALE + bias_new
        m = jnp.maximum(jnp.max(s, axis=-1, keepdims=True), s_new)
        p = jnp.exp(s - m)
        p_new = jnp.exp(s_new - m)
        l = jnp.sum(p, axis=-1, keepdims=True) + p_new
        p16 = p.astype(BF16)
        pn = p_new.astype(BF16).astype(F32)
        lse = m + jnp.log(l)
        for h in range(GROUP_HEADS):
            hs = slice(h * B_HEAD_DIM, (h + 1) * B_HEAD_DIM)
            oh = (_dot(p16, cached(b, 2 * h + 1).astype(BF16)) + pn * vn[:, hs]) / l
            o_ref[b:b + 1, hs] = jnp.sum(jnp.where(row == h, oh, 0.0), axis=0, keepdims=True)
            lse_ref[b:b + 1, hs] = jnp.sum(jnp.where(row == h, lse, 0.0), axis=0, keepdims=True)


def _dil_sample(q_s, kv_s, buf, bias, g, dil):
    db, length = buf.shape[:2]
    bb = DIL_SAMPLE_BLOCK
    rows = jnp.transpose(buf, (0, 1, 3, 2, 4))
    if dil == 1:
        rows = rows.reshape(db, length * BUF_ROWS, B_HEAD_DIM)
        buf_spec = pl.BlockSpec((bb, length * BUF_ROWS, B_HEAD_DIM), lambda i: (i, 0, 0))
    else:
        rows = rows.reshape(db, length // dil, dil * BUF_ROWS, B_HEAD_DIM)
        buf_spec = pl.BlockSpec((bb, DIL_KEYS, BUF_ROWS_PAD, B_HEAD_DIM), lambda i: (i, 0, 0, 0))
    return pl.pallas_call(
        functools.partial(_dil_sample_kernel, dil=dil),
        grid=(db // bb,),
        in_specs=[
            pl.BlockSpec((bb, GROUP_DIM), lambda i: (i, g)),
            pl.BlockSpec((bb, GROUP_DIM), lambda i: (i, g)),
            pl.BlockSpec((bb, GROUP_DIM), lambda i: (i, N_GROUPS + g)),
            buf_spec,
            pl.BlockSpec((SUBLANES, 2 * DIL_KEYS), lambda i: (0, 0)),
        ],
        out_specs=[
            pl.BlockSpec((bb, GROUP_DIM), lambda i: (i, 0)),
            pl.BlockSpec((bb, GROUP_DIM), lambda i: (i, 0)),
        ],
        out_shape=[
            jax.ShapeDtypeStruct((db, GROUP_DIM), F32),
            jax.ShapeDtypeStruct((db, GROUP_DIM), F32),
        ],
        compiler_params=_cparams("parallel"),
        name="dilated_sample",
    )(q_s, kv_s, kv_s, rows, bias)


def _merge_out_kernel(o0, o1, o2, l0, l1, l2, w_ref, x_ref, g_ref, out_ref, merged_ref, *scratch, dils):
    srcs = ((o0, l0), (o1, l1), (o2, l2))
    tm = x_ref.shape[0]
    ordered = []
    si = 0
    for (o_src, l_src), dil in zip(srcs, dils):
        if dil == 1:
            ordered.append(None)
            continue
        o_dst, l_dst = scratch[si], scratch[si + 1]
        si += 2
        n = tm // dil
        for r in range(dil):
            for h in range(GROUP_HEADS):
                lo = r * GROUP_DIM + h * B_HEAD_DIM
                o_dst[h, pl.ds(r, n, stride=dil), :] = o_src[:, lo:lo + B_HEAD_DIM]
                l_dst[h, pl.ds(r, n, stride=dil), :] = l_src[:, lo:lo + B_HEAD_DIM]
        ordered.append((o_dst, l_dst))
    for h in range(GROUP_HEADS):
        hs = slice(h * B_HEAD_DIM, (h + 1) * B_HEAD_DIM)
        os_, ls_ = [], []
        for (o_src, l_src), dst in zip(srcs, ordered):
            os_.append(o_src[:, hs] if dst is None else dst[0][h])
            ls_.append(l_src[:, hs] if dst is None else dst[1][h])
        mx = jnp.maximum(jnp.maximum(ls_[0], ls_[1]), ls_[2])
        e0, e1, e2 = jnp.exp(ls_[0] - mx), jnp.exp(ls_[1] - mx), jnp.exp(ls_[2] - mx)
        merged = (e0 * os_[0] + e1 * os_[1] + e2 * os_[2]) / (e0 + e1 + e2)
        merged_ref[:, hs] = merged.astype(BF16)
    y = _dot(merged_ref[...], w_ref[...])
    out_ref[...] = x_ref[...] + _rms(y, g_ref[...])


def _merge_out(outs, lses, dils, w, x, g, tm):
    m, d = x.shape
    parts = [pl.BlockSpec((tm // dil, dil * GROUP_DIM), lambda i: (i, 0)) for dil in dils]
    scratch = [pltpu.VMEM((tm, GROUP_DIM), BF16)]
    for dil in dils:
        if dil > 1:
            scratch += [pltpu.VMEM((GROUP_HEADS, tm, B_HEAD_DIM), F32)] * 2
    return pl.pallas_call(
        functools.partial(_merge_out_kernel, dils=tuple(dils)),
        grid=(m // tm,),
        in_specs=parts + parts + [
            pl.BlockSpec((GROUP_DIM, d), lambda i: (0, 0)),
            pl.BlockSpec((tm, d), lambda i: (i, 0)),
            pl.BlockSpec((1, d), lambda i: (0, 0)),
        ],
        out_specs=pl.BlockSpec((tm, d), lambda i: (i, 0)),
        out_shape=jax.ShapeDtypeStruct((m, d), F32),
        scratch_shapes=scratch,
        compiler_params=_cparams("parallel"),
        name="dilated_merge_out",
    )(*outs, *lses, w, x, g)


def _t5_bucket(dist):
    max_exact = T5_BUCKETS // 2
    d = np.maximum(dist, 1).astype(np.float32)
    large = max_exact + (np.log(d / max_exact) / math.log(T5_MAX_DIST / max_exact)
                         * (T5_BUCKETS - max_exact)).astype(np.int32)
    large = np.minimum(large, T5_BUCKETS - 1)
    return np.where(dist < max_exact, dist, large).astype(np.int32)


def _rows_static(table, idx):
    pieces, start = [], 0
    for i in range(1, len(idx) + 1):
        if i == len(idx) or idx[i] != idx[start]:
            row = table[int(idx[start]):int(idx[start]) + 1]
            pieces.append(jnp.broadcast_to(row, (i - start,) + table.shape[1:]))
            start = i
    return jnp.concatenate(pieces, axis=0)


def _group_bias(t5_bias, g):
    window, dil = DIL_GROUPS[g]
    buckets = _t5_bucket(dil * np.arange(window // dil + 1))
    return _rows_static(t5_bias[:, g * GROUP_HEADS:(g + 1) * GROUP_HEADS].astype(F32), buckets)


def _prompt_bias(bias):
    n = 2 * DIL_TILE
    heads = bias.shape[1]
    period = jnp.concatenate([bias[::-1], jnp.full((n - DIL_KEYS, heads), NEG_INF, F32)], axis=0)
    band = jnp.tile(period, (DIL_TILE, 1))[:DIL_TILE * n].reshape(DIL_TILE, n, heads)
    return jnp.transpose(band, (2, 0, 1))


def _sample_bias(bias):
    keys = bias[1:][::-1].T
    own = jnp.broadcast_to(bias[0][:, None], (GROUP_HEADS, DIL_KEYS))
    both = jnp.concatenate([keys, own], axis=1)
    return jnp.pad(both, ((0, SUBLANES - GROUP_HEADS), (0, 0)))


def _rope_tables(pos):
    half = ROPE_DIM // 2
    inv = ROPE_THETA ** (-jnp.arange(half, dtype=F32) / half)
    ang = pos.astype(F32)[:, None] * inv
    cos, sin = jnp.cos(ang), jnp.sin(ang)
    zero = jnp.zeros((pos.shape[0], LANES - ROPE_DIM), F32)
    return (jnp.concatenate([cos, cos, zero], axis=1), jnp.concatenate([-sin, sin, zero], axis=1))


def _tile(n, pref):
    return pref if n % pref == 0 else n


def kernel(x_prompt, x_sample, cache_mla, page_table, cache_b_win0, cache_b_win1, cache_b_win2,
           norm_gains, w_mla_in, g_q_a, g_kv_a, w_uq, w_uk, w_uv, w_mla_o,
           g_kv_b, w_kv_b, w_q_b, w_b_o, t5_bias, w_ffn_in, w_ffn_out):
    batch, seq, d = x_prompt.shape
    db = x_sample.shape[0]
    assert x_sample.shape[1] == 1 and DEPTH == 2 and N_A_LAYERS == 1
    win_bufs = (cache_b_win0, cache_b_win1, cache_b_win2)
    for gi, (window, dil) in enumerate(DIL_GROUPS):
        assert win_bufs[gi].shape[1] == window and window // dil == DIL_KEYS
        assert seq % (dil * DIL_TILE) == 0
    mp = batch * seq
    xp = x_prompt.reshape(mp, d)
    xs = x_sample.reshape(db, d)
    gains = norm_gains.reshape(DEPTH, 4, 1, d)
    tm_p = _tile(mp, 1024)
    tm_h = _tile(mp, 512)

    rope_cols = jnp.pad(w_mla_in[0][:, Q_LORA + KV_LORA:], ((0, 0), (0, LANES - ROPE_DIM)))
    w_in_pad = jnp.concatenate([w_mla_in[0][:, :Q_LORA + KV_LORA], rope_cols], axis=1).astype(BF16)
    w_uq_pad = jnp.pad(w_uq[0].reshape(Q_LORA, MLA_HEADS, NOPE_DIM + ROPE_DIM),
                       ((0, 0), (0, 0), (0, HEAD_PAD - NOPE_DIM - ROPE_DIM)))
    w_uq_pad = w_uq_pad.reshape(Q_LORA, MLA_HEADS * HEAD_PAD).astype(BF16)
    w_k_lat = jnp.pad(w_uk[0], ((0, 0), (0, 0), (0, HEAD_PAD - NOPE_DIM)))
    rope_copy = jnp.pad(jnp.eye(ROPE_DIM, dtype=F32), ((0, 0), (NOPE_DIM, HEAD_PAD - NOPE_DIM - ROPE_DIM)))
    rope_copy = jnp.broadcast_to(rope_copy[:, None, :], (ROPE_DIM, MLA_HEADS, HEAD_PAD))
    w_k = jnp.concatenate([w_k_lat, rope_copy], axis=0).reshape(ROW_DIM, MLA_HEADS * HEAD_PAD)
    w_v = jnp.pad(w_uv[0].reshape(KV_LORA, MLA_HEADS * V_DIM), ((0, ROPE_DIM), (0, 0)))
    w_kv = jnp.concatenate([w_k, w_v], axis=1).astype(BF16)
    w_uk_t = jnp.transpose(w_uk[0], (1, 2, 0)).astype(BF16)
    w_uv_t = jnp.transpose(w_uv[0], (1, 0, 2)).astype(BF16)
    w_o = w_mla_o[0].astype(BF16)
    w_kvb = w_kv_b.astype(BF16)
    w_qb = w_q_b[0].astype(BF16)
    w_bo = w_b_o[0].astype(BF16)
    w_f_in = w_ffn_in.astype(BF16)
    w_f_out = w_ffn_out.astype(BF16)
    f = w_ffn_out.shape[1]
    tf = _tile(f, 512)

    c_p, s_p = _rope_tables(jnp.tile(jnp.arange(seq), batch))
    c_s, s_s = _rope_tables(jnp.full((db,), PAST_LEN, jnp.int32))

    g = gains[0]
    cq_p, rows_p = _mla_in(xp, g[0], w_in_pad, g_q_a[0:1], g_kv_a[0:1], c_p, s_p, tm_h)
    cq_s, rows_s = _mla_in(xs, g[0], w_in_pad, g_q_a[0:1], g_kv_a[0:1], c_s, s_s, db)
    q_p = _q_up(cq_p, w_uq_pad, c_p, s_p, tm_p, 1024)
    q_s = _q_up(cq_s, w_uq_pad, c_s, s_s, db, 1024)

    kv_p = _cast_mm(rows_p, w_kv, BF16, tm_p, 1024)
    attn_p = _mla_prompt_attention(q_p, kv_p, batch, seq, _tile(seq, 512))
    xp = _mm_norm_res(attn_p, w_o, xp, g[1], tm_p, 512)

    q_lat = _q_absorb(q_s, w_uk_t)
    q_rope_s = q_s.reshape(db, MLA_HEADS, HEAD_PAD)[:, :, NOPE_DIM:NOPE_DIM + ROPE_DIM]
    q_cat = jnp.concatenate([jnp.transpose(q_lat, (1, 0, 2)), q_rope_s], axis=-1)
    cache_t = jnp.swapaxes(cache_mla[0], 1, 2)
    o_lat = _mla_decode(q_cat, rows_s, cache_t, page_table)
    attn_s = _v_up(jnp.transpose(o_lat, (1, 0, 2)).astype(BF16), w_uv_t)
    xs = _mm_norm_res(attn_s, w_o, xs, g[1], db, 512)

    xp = _ffn(xp, g[2], g[3], w_f_in[0], w_f_out[0], tm_h, tf)
    xs = _ffn(xs, g[2], g[3], w_f_in[0], w_f_out[0], db, tf)

    g = gains[1]
    g_kvb = g_kv_b.reshape(1, d)
    kvb_s = _norm_mm(xs, g_kvb, w_kvb, F32, db, 512)
    qb_s = _norm_mm(xs, g[0], w_qb, F32, db, GROUP_DIM)
    outs_p, lses_p, outs_s, lses_s, kv_groups = [], [], [], [], []
    dils = [dil for _, dil in DIL_GROUPS]
    biases = [_group_bias(t5_bias, gi) for gi in range(N_GROUPS)]
    band_all = _prompt_bias(jnp.concatenate(biases, axis=1))
    for gi, (window, dil) in enumerate(DIL_GROUPS):
        cols = slice(gi * GROUP_DIM, (gi + 1) * GROUP_DIM)
        w_g = jnp.concatenate([w_qb[:, cols], w_kvb[:, cols],
                               w_kvb[:, N_GROUPS * GROUP_DIM:][:, cols]], axis=1)
        qkv, kv_g = _dil_proj(xp, g[0], g_kvb, w_g, dil, tm_h)
        kv_groups.append(kv_g)
        bias = biases[gi]
        o, lse = _dil_prompt(qkv, band_all[gi * GROUP_HEADS:(gi + 1) * GROUP_HEADS], dil, batch, seq)
        outs_p.append(o)
        lses_p.append(lse)
        o, lse = _dil_sample(qb_s, kvb_s, win_bufs[gi], _sample_bias(bias), gi, dil)
        outs_s.append(o)
        lses_s.append(lse)
    xp = _merge_out(outs_p, lses_p, dils, w_bo, xp, g[1], _tile(mp, 256))
    xs = _merge_out(outs_s, lses_s, [1] * N_GROUPS, w_bo, xs, g[1], db)

    xp = _ffn(xp, g[2], g[3], w_f_in[1], w_f_out[1], tm_h, tf)
    xs = _ffn(xs, g[2], g[3], w_f_in[1], w_f_out[1], db, tf)

    kv6_s = kvb_s.reshape(db, 1, 2, N_GROUPS, GROUP_HEADS, B_HEAD_DIM)
    wins = []
    for gi, (window, _) in enumerate(DIL_GROUPS):
        kv5 = kv_groups[gi].reshape(batch, seq, 2, GROUP_HEADS, B_HEAD_DIM)
        wins.append(kv5[:, seq - min(window, seq):])
        wins.append(kv6_s[:, :, :, gi])
    return (xp.reshape(batch, seq, d), xs.reshape(db, 1, d),
            rows_p.reshape(1, batch, seq, ROW_DIM), rows_s.reshape(1, db, 1, ROW_DIM), *wins)
```

```python
import functools
import math

import numpy as np
import jax
import jax.numpy as jnp
from jax import lax
from jax.experimental import pallas as pl
from jax.experimental.pallas import tpu as pltpu

D_MODEL = 2048
DEPTH = 2
PAST_LEN = 16384
PAGE_SIZE = 128
N_A_LAYERS = DEPTH // 2
MLA_HEADS = 16
Q_LORA = 512
KV_LORA = 512
NOPE_DIM = 128
ROPE_DIM = 64
V_DIM = 128
ROPE_THETA = 10000.0
MLA_SCALE = (NOPE_DIM + ROPE_DIM) ** -0.5
DIL_GROUPS = ((128, 1), (512, 4), (2048, 16))
N_GROUPS = len(DIL_GROUPS)
GROUP_HEADS = 6
B_HEAD_DIM = 128
B_HEADS = N_GROUPS * GROUP_HEADS
B_SCALE = B_HEAD_DIM ** -0.5
T5_BUCKETS = 32
T5_MAX_DIST = 2048
EPS = 1e-6
NEG_INF = -1e30

F32 = jnp.float32
BF16 = jnp.bfloat16

VMEM_LIMIT_BYTES = 56 * 1024 * 1024
LANES = 128
SUBLANES = 8
HEAD_PAD = 2 * LANES
ROW_DIM = KV_LORA + ROPE_DIM
GROUP_DIM = GROUP_HEADS * B_HEAD_DIM
QKV_DIM = 3 * GROUP_DIM
DIL_KEYS = 128
DIL_TILE = 128
FLASH_HEADS = 2
DEC_PAGES = 16
DEC_SLOTS = 3
DIL_SAMPLE_BLOCK = 8
BUF_ROWS = 2 * GROUP_HEADS
BUF_ROWS_PAD = 16


def _cparams(*semantics):
    return pltpu.CompilerParams(dimension_semantics=semantics, vmem_limit_bytes=VMEM_LIMIT_BYTES)


def _rms(x, g):
    return x * lax.rsqrt(jnp.mean(x * x, axis=-1, keepdims=True) + EPS) * g


def _dot(a, b):
    return jnp.dot(a, b, preferred_element_type=F32)


def _dot_nt(a, b):
    return lax.dot_general(a, b, (((1,), (1,)), ((), ())), preferred_element_type=F32)


def _dot_tn(a, b):
    return lax.dot_general(a, b, (((0,), (0,)), ((), ())), preferred_element_type=F32)


def _rope128(x, c, s):
    lane = lax.broadcasted_iota(jnp.int32, x.shape, 1)
    half = ROPE_DIM // 2
    swapped = jnp.where(lane < half, pltpu.roll(x, LANES - half, 1), pltpu.roll(x, half, 1))
    return x * c + swapped * s


def _mla_in_kernel(x_ref, g_ref, w_ref, gq_ref, gkv_ref, c_ref, s_ref, cq_ref, rows_ref):
    xn = _rms(x_ref[...], g_ref[...]).astype(BF16)
    a = _dot(xn, w_ref[...])
    cq_ref[...] = _rms(a[:, :Q_LORA], gq_ref[...]).astype(BF16)
    rows_ref[:, :KV_LORA] = _rms(a[:, Q_LORA:Q_LORA + KV_LORA], gkv_ref[...])
    kr = _rope128(a[:, Q_LORA + KV_LORA:], c_ref[...], s_ref[...])
    rows_ref[:, KV_LORA:] = kr[:, :ROPE_DIM]


def _mla_in(x, g, w_pad, gq, gkv, c_tab, s_tab, tm):
    m, d = x.shape
    n = w_pad.shape[1]
    return pl.pallas_call(
        _mla_in_kernel,
        grid=(m // tm,),
        in_specs=[
            pl.BlockSpec((tm, d), lambda i: (i, 0)),
            pl.BlockSpec((1, d), lambda i: (0, 0)),
            pl.BlockSpec((d, n), lambda i: (0, 0)),
            pl.BlockSpec((1, Q_LORA), lambda i: (0, 0)),
            pl.BlockSpec((1, KV_LORA), lambda i: (0, 0)),
            pl.BlockSpec((tm, LANES), lambda i: (i, 0)),
            pl.BlockSpec((tm, LANES), lambda i: (i, 0)),
        ],
        out_specs=[
            pl.BlockSpec((tm, Q_LORA), lambda i: (i, 0)),
            pl.BlockSpec((tm, ROW_DIM), lambda i: (i, 0)),
        ],
        out_shape=[
            jax.ShapeDtypeStruct((m, Q_LORA), BF16),
            jax.ShapeDtypeStruct((m, ROW_DIM), F32),
        ],
        compiler_params=_cparams("parallel"),
        name="mla_in",
    )(x, g, w_pad, gq, gkv, c_tab, s_tab)


def _q_up_kernel(cq_ref, w_ref, c_ref, s_ref, o_ref, *, heads_per_tile):
    a = _dot(cq_ref[...], w_ref[...])
    c = c_ref[...]
    s = s_ref[...]
    for h in range(heads_per_tile):
        lo = h * HEAD_PAD
        o_ref[:, lo:lo + LANES] = a[:, lo:lo + LANES].astype(BF16)
        o_ref[:, lo + LANES:lo + HEAD_PAD] = _rope128(a[:, lo + LANES:lo + HEAD_PAD], c, s).astype(BF16)


def _q_up(cq, w_pad, c_tab, s_tab, tm, tn):
    m, k = cq.shape
    n = w_pad.shape[1]
    return pl.pallas_call(
        functools.partial(_q_up_kernel, heads_per_tile=tn // HEAD_PAD),
        grid=(m // tm, n // tn),
        in_specs=[
            pl.BlockSpec((tm, k), lambda i, j: (i, 0)),
            pl.BlockSpec((k, tn), lambda i, j: (0, j)),
            pl.BlockSpec((tm, LANES), lambda i, j: (i, 0)),
            pl.BlockSpec((tm, LANES), lambda i, j: (i, 0)),
        ],
        out_specs=pl.BlockSpec((tm, tn), lambda i, j: (i, j)),
        out_shape=jax.ShapeDtypeStruct((m, n), BF16),
        compiler_params=_cparams("parallel", "parallel"),
        name="mla_q_up",
    )(cq, w_pad, c_tab, s_tab)


def _cast_mm_kernel(x_ref, w_ref, o_ref):
    o_ref[...] = _dot(x_ref[...].astype(BF16), w_ref[...]).astype(o_ref.dtype)


def _cast_mm(x, w, out_dtype, tm, tn):
    m, k = x.shape
    n = w.shape[1]
    return pl.pallas_call(
        _cast_mm_kernel,
        grid=(m // tm, n // tn),
        in_specs=[
            pl.BlockSpec((tm, k), lambda i, j: (i, 0)),
            pl.BlockSpec((k, tn), lambda i, j: (0, j)),
        ],
        out_specs=pl.BlockSpec((tm, tn), lambda i, j: (i, j)),
        out_shape=jax.ShapeDtypeStruct((m, n), out_dtype),
        compiler_params=_cparams("parallel", "parallel"),
        name="cast_mm",
    )(x, w)


def _norm_mm_kernel(x_ref, g_ref, w_ref, o_ref, xn_ref):
    @pl.when(pl.program_id(1) == 0)
    def _():
        xn_ref[...] = _rms(x_ref[...], g_ref[...]).astype(BF16)

    o_ref[...] = _dot(xn_ref[...], w_ref[...]).astype(o_ref.dtype)


def _norm_mm(x, g, w, out_dtype, tm, tn):
    m, d = x.shape
    n = w.shape[1]
    return pl.pallas_call(
        _norm_mm_kernel,
        grid=(m // tm, n // tn),
        in_specs=[
            pl.BlockSpec((tm, d), lambda i, j: (i, 0)),
            pl.BlockSpec((1, d), lambda i, j: (0, 0)),
            pl.BlockSpec((d, tn), lambda i, j: (0, j)),
        ],
        out_specs=pl.BlockSpec((tm, tn), lambda i, j: (i, j)),
        out_shape=jax.ShapeDtypeStruct((m, n), out_dtype),
        scratch_shapes=[pltpu.VMEM((tm, d), BF16)],
        compiler_params=_cparams("parallel", "arbitrary"),
        name="norm_mm",
    )(x, g, w)


def _mm_norm_res_kernel(a_ref, w_ref, x_ref, g_ref, o_ref, *, nk):
    k = pl.program_id(1)
    part = _dot(a_ref[...], w_ref[...])

    @pl.when(k == 0)
    def _():
        o_ref[...] = part

    @pl.when(k > 0)
    def _():
        o_ref[...] += part

    @pl.when(k == nk - 1)
    def _():
        o_ref[...] = x_ref[...] + _rms(o_ref[...], g_ref[...])


def _mm_norm_res(a, w, x, g, tm, tk):
    m, kdim = a.shape
    d = w.shape[1]
    nk = kdim // tk
    return pl.pallas_call(
        functools.partial(_mm_norm_res_kernel, nk=nk),
        grid=(m // tm, nk),
        in_specs=[
            pl.BlockSpec((tm, tk), lambda i, k: (i, k)),
            pl.BlockSpec((tk, d), lambda i, k: (k, 0)),
            pl.BlockSpec((tm, d), lambda i, k: (i, 0)),
            pl.BlockSpec((1, d), lambda i, k: (0, 0)),
        ],
        out_specs=pl.BlockSpec((tm, d), lambda i, k: (i, 0)),
        out_shape=jax.ShapeDtypeStruct((m, d), F32),
        compiler_params=_cparams("parallel", "arbitrary"),
        name="mm_norm_res",
    )(a, w, x, g)


def _ffn_kernel(x_ref, g_in_ref, g_out_ref, wg_ref, wu_ref, wo_ref, o_ref, xn_ref, *, nf):
    j = pl.program_id(1)

    @pl.when(j == 0)
    def _():
        xn_ref[...] = _rms(x_ref[...], g_in_ref[...]).astype(BF16)

    xn = xn_ref[...]
    tf = wo_ref.shape[0]
    half = tf // 2 if tf % (2 * LANES) == 0 else tf
    part = None
    for lo in range(0, tf, half):
        gate = _dot(xn, wg_ref[:, lo:lo + half])
        up = _dot(xn, wu_ref[:, lo:lo + half])
        act = (gate * jax.nn.sigmoid(gate) * up).astype(BF16)
        down = _dot(act, wo_ref[lo:lo + half, :])
        part = down if part is None else part + down

    @pl.when(j == 0)
    def _():
        o_ref[...] = part

    @pl.when(j > 0)
    def _():
        o_ref[...] += part

    @pl.when(j == nf - 1)
    def _():
        o_ref[...] = x_ref[...] + _rms(o_ref[...], g_out_ref[...])


def _ffn(x, g_in, g_out, w_in, w_out, tm, tf):
    m, d = x.shape
    f = w_out.shape[0]
    nf = f // tf
    return pl.pallas_call(
        functools.partial(_ffn_kernel, nf=nf),
        grid=(m // tm, nf),
        in_specs=[
            pl.BlockSpec((tm, d), lambda i, j: (i, 0)),
            pl.BlockSpec((1, d), lambda i, j: (0, 0)),
            pl.BlockSpec((1, d), lambda i, j: (0, 0)),
            pl.BlockSpec((d, tf), lambda i, j: (0, j)),
            pl.BlockSpec((d, tf), lambda i, j: (0, j + nf)),
            pl.BlockSpec((tf, d), lambda i, j: (j, 0)),
        ],
        out_specs=pl.BlockSpec((tm, d), lambda i, j: (i, 0)),
        out_shape=jax.ShapeDtypeStruct((m, d), F32),
        scratch_shapes=[pltpu.VMEM((tm, d), BF16)],
        compiler_params=_cparams("parallel", "arbitrary"),
        name="ffn",
    )(x, g_in, g_out, w_in, w_in, w_out)


def _flash_kernel(q_ref, k_ref, v_ref, o_ref, m_ref, l_ref, acc_ref, *, tq):
    qi = pl.program_id(2)
    heads = FLASH_HEADS
    qts = [q_ref[:, h * HEAD_PAD:(h + 1) * HEAD_PAD].astype(F32).T.astype(BF16) for h in range(heads)]
    m_ref[...] = jnp.full(m_ref.shape, -jnp.inf, F32)
    l_ref[...] = jnp.zeros(l_ref.shape, F32)
    acc_ref[...] = jnp.zeros(acc_ref.shape, F32)

    def step(kc, diagonal):
        start = pl.multiple_of(kc * tq, tq)
        for h in range(heads):
            k = k_ref[pl.ds(start, tq), h * HEAD_PAD:(h + 1) * HEAD_PAD]
            v = v_ref[pl.ds(start, tq), h * V_DIM:(h + 1) * V_DIM]
            s = _dot(k, qts[h]) * MLA_SCALE
            if diagonal:
                key = lax.broadcasted_iota(jnp.int32, s.shape, 0)
                query = lax.broadcasted_iota(jnp.int32, s.shape, 1)
                s = jnp.where(key <= query, s, NEG_INF)
            m_prev = m_ref[h]
            m_new = jnp.maximum(m_prev, jnp.max(s, axis=0, keepdims=True))
            alpha = jnp.exp(m_prev - m_new)
            p = jnp.exp(s - m_new)
            l_ref[h] = alpha * l_ref[h] + jnp.sum(p, axis=0, keepdims=True)
            acc_ref[h] = alpha * acc_ref[h] + _dot_tn(v, p.astype(BF16))
            m_ref[h] = m_new

    def body(kc, carry):
        step(kc, False)
        return carry

    lax.fori_loop(0, qi, body, 0)
    step(qi, True)
    for h in range(heads):
        o_ref[:, h * V_DIM:(h + 1) * V_DIM] = (acc_ref[h] / l_ref[h]).T.astype(o_ref.dtype)


def _mla_prompt_attention(q, kv, batch, seq, tq):
    nq = seq // tq
    hb = FLASH_HEADS
    v_block0 = MLA_HEADS * HEAD_PAD // (hb * V_DIM)
    return pl.pallas_call(
        functools.partial(_flash_kernel, tq=tq),
        grid=(batch, MLA_HEADS // hb, nq),
        in_specs=[
            pl.BlockSpec((tq, hb * HEAD_PAD), lambda b, h, i: (b * nq + i, h)),
            pl.BlockSpec((seq, hb * HEAD_PAD), lambda b, h, i: (b, h)),
            pl.BlockSpec((seq, hb * V_DIM), lambda b, h, i: (b, v_block0 + h)),
        ],
        out_specs=pl.BlockSpec((tq, hb * V_DIM), lambda b, h, i: (b * nq + i, h)),
        out_shape=jax.ShapeDtypeStruct((batch * seq, MLA_HEADS * V_DIM), BF16),
        scratch_shapes=[
            pltpu.VMEM((hb, 1, tq), F32),
            pltpu.VMEM((hb, 1, tq), F32),
            pltpu.VMEM((hb, V_DIM, tq), F32),
        ],
        compiler_params=_cparams("parallel", "parallel", "arbitrary"),
        name="mla_prompt_attention",
    )(q, kv, kv)


def _head_mm_kernel(a_ref, w_ref, o_ref):
    o_ref[...] = _dot(a_ref[...], w_ref[...]).astype(o_ref.dtype)


def _q_absorb(q_s, w_uk_t):
    db = q_s.shape[0]
    return pl.pallas_call(
        _head_mm_kernel,
        grid=(MLA_HEADS,),
        in_specs=[
            pl.BlockSpec((db, NOPE_DIM), lambda h: (0, 2 * h)),
            pl.BlockSpec((None, NOPE_DIM, KV_LORA), lambda h: (h, 0, 0)),
        ],
        out_specs=pl.BlockSpec((None, db, KV_LORA), lambda h: (h, 0, 0)),
        out_shape=jax.ShapeDtypeStruct((MLA_HEADS, db, KV_LORA), BF16),
        compiler_params=_cparams("parallel"),
        name="mla_q_absorb",
    )(q_s, w_uk_t)


def _v_up(o_lat_t, w_uv_t):
    db = o_lat_t.shape[1]
    return pl.pallas_call(
        _head_mm_kernel,
        grid=(MLA_HEADS,),
        in_specs=[
            pl.BlockSpec((None, db, KV_LORA), lambda h: (h, 0, 0)),
            pl.BlockSpec((None, KV_LORA, V_DIM), lambda h: (h, 0, 0)),
        ],
        out_specs=pl.BlockSpec((db, V_DIM), lambda h: (0, h)),
        out_shape=jax.ShapeDtypeStruct((db, MLA_HEADS * V_DIM), BF16),
        compiler_params=_cparams("parallel"),
        name="mla_v_up",
    )(o_lat_t, w_uv_t)


def _mla_decode_kernel(pt_ref, q_ref, new_ref, cache_ref, o_ref, pages_ref, kvt0_ref, kvt1_ref, s0_ref, s1_ref,
                       m_ref, l_ref, acc_ref, sem, *, n_pages, n_chunks, n_steps):
    kvt_refs = (kvt0_ref, kvt1_ref)
    s_refs = (s0_ref, s1_ref)
    t = pl.program_id(0)
    prev_chunk = (t + n_chunks - 1) % n_chunks

    def page_copy(step, p):
        chunk = jnp.minimum(step, n_steps - 2)
        slot = step % DEC_SLOTS
        return pltpu.make_async_copy(cache_ref.at[pt_ref[chunk * n_pages + p]], pages_ref.at[slot, p],
                                     sem.at[slot, p])

    def start_chunk(step):
        for p in range(n_pages):
            page_copy(step, p).start(priority=p % 2)

    @pl.when(t == 0)
    def _():
        for step in range(DEC_SLOTS - 1):
            start_chunk(step)
        kvt_refs[1][...] = jnp.zeros(kvt_refs[1].shape, BF16)
        s_refs[1][...] = jnp.zeros(s_refs[1].shape, F32)
        m_ref[...] = jnp.zeros(m_ref.shape, F32)
        l_ref[...] = jnp.zeros(l_ref.shape, F32)
        acc_ref[...] = jnp.zeros(acc_ref.shape, F32)

    @pl.when(jnp.logical_and(t > 0, prev_chunk == 0))
    def _():
        new = new_ref[...].astype(BF16).astype(F32)
        m_ref[...] = jnp.sum(q_ref[...].astype(F32) * new, axis=-1, keepdims=True) * MLA_SCALE
        l_ref[...] = jnp.ones(l_ref.shape, F32)
        acc_ref[...] = jnp.broadcast_to(new, acc_ref.shape)

    @pl.when(t + DEC_SLOTS - 1 < n_steps)
    def _():
        start_chunk(t + DEC_SLOTS - 1)

    def work(cur, prev):
        kvt_cur, kvt_prev = kvt_refs[cur], kvt_refs[prev]
        slot = t % DEC_SLOTS
        for p in range(n_pages):
            page_copy(t, p).wait()
            kvt_cur[:, p * PAGE_SIZE:(p + 1) * PAGE_SIZE] = pages_ref[slot, p].astype(BF16)
        s_refs[cur][...] = _dot(q_ref[...], kvt_cur[...]) * MLA_SCALE
        s = s_refs[prev][...]
        m_prev = m_ref[...]
        m_new = jnp.maximum(m_prev, jnp.max(s, axis=-1, keepdims=True))
        alpha = jnp.exp(m_prev - m_new)
        p = jnp.exp(s - m_new)
        l_ref[...] = alpha * l_ref[...] + jnp.sum(p, axis=-1, keepdims=True)
        acc_ref[...] = alpha * acc_ref[...] + _dot_nt(p.astype(BF16), kvt_prev[...])
        m_ref[...] = m_new

    @pl.when(t % 2 == 0)
    def _():
        work(0, 1)

    @pl.when(t % 2 == 1)
    def _():
        work(1, 0)

    @pl.when(jnp.logical_and(t > 0, prev_chunk == n_chunks - 1))
    def _():
        o_ref[...] = (acc_ref[...] / l_ref[...])[:, :KV_LORA]


def _mla_decode(q_cat, rows_new, cache_t, page_table):
    db = q_cat.shape[0]
    n_tab = page_table.shape[1]
    n_pages = min(DEC_PAGES, n_tab // 2)
    n_chunks = n_tab // n_pages
    assert n_chunks >= 2 and n_chunks * n_pages == n_tab
    n_steps = db * n_chunks + 1
    width = n_pages * PAGE_SIZE
    chunk = lambda t: jnp.minimum(t, n_steps - 2)
    batch_of = lambda t: chunk(t) // n_chunks

    grid_spec = pltpu.PrefetchScalarGridSpec(
        num_scalar_prefetch=1,
        grid=(n_steps,),
        in_specs=[
            pl.BlockSpec((None, MLA_HEADS, ROW_DIM), lambda t, pt: (batch_of(t), 0, 0)),
            pl.BlockSpec((None, 1, ROW_DIM), lambda t, pt: (batch_of(t), 0, 0)),
            pl.BlockSpec(memory_space=pl.ANY),
        ],
        out_specs=pl.BlockSpec((None, MLA_HEADS, KV_LORA),
                               lambda t, pt: (jnp.maximum(t - 1, 0) // n_chunks, 0, 0)),
        scratch_shapes=[
            pltpu.VMEM((DEC_SLOTS, n_pages, ROW_DIM, PAGE_SIZE), F32),
            pltpu.VMEM((ROW_DIM, width), BF16),
            pltpu.VMEM((ROW_DIM, width), BF16),
            pltpu.VMEM((MLA_HEADS, width), F32),
            pltpu.VMEM((MLA_HEADS, width), F32),
            pltpu.VMEM((MLA_HEADS, 1), F32),
            pltpu.VMEM((MLA_HEADS, 1), F32),
            pltpu.VMEM((MLA_HEADS, ROW_DIM), F32),
            pltpu.SemaphoreType.DMA((DEC_SLOTS, n_pages)),
        ],
    )
    return pl.pallas_call(
        functools.partial(_mla_decode_kernel, n_pages=n_pages, n_chunks=n_chunks, n_steps=n_steps),
        grid_spec=grid_spec,
        out_shape=jax.ShapeDtypeStruct((db, MLA_HEADS, KV_LORA), F32),
        compiler_params=_cparams("arbitrary"),
        name="mla_decode",
    )(page_table.reshape(-1), q_cat, rows_new.reshape(db, 1, ROW_DIM), cache_t)


def _dil_proj_kernel(x_ref, gq_ref, gkv_ref, w_ref, qkv_ref, kv_ref, y_ref, *, dil):
    x = x_ref[...]
    xr = x * lax.rsqrt(jnp.mean(x * x, axis=-1, keepdims=True) + EPS)
    xq = (xr * gq_ref[...]).astype(BF16)
    xkv = (xr * gkv_ref[...]).astype(BF16)
    n = x.shape[0] // dil
    for j in range(3):
        y = _dot(xq if j == 0 else xkv, w_ref[:, j * GROUP_DIM:(j + 1) * GROUP_DIM])
        if j > 0:
            kv_ref[:, (j - 1) * GROUP_DIM:j * GROUP_DIM] = y
        if dil == 1:
            qkv_ref[:, j * GROUP_DIM:(j + 1) * GROUP_DIM] = y.astype(BF16)
            continue
        for h in range(GROUP_HEADS):
            y_ref[h] = y[:, h * B_HEAD_DIM:(h + 1) * B_HEAD_DIM]
        for r in range(dil):
            for h in range(GROUP_HEADS):
                lo = r * QKV_DIM + j * GROUP_DIM + h * B_HEAD_DIM
                qkv_ref[:, lo:lo + B_HEAD_DIM] = y_ref[h, pl.ds(r, n, stride=dil), :].astype(BF16)


def _dil_proj(x, gq, gkv, w, dil, tm):
    m, d = x.shape
    return pl.pallas_call(
        functools.partial(_dil_proj_kernel, dil=dil),
        grid=(m // tm,),
        in_specs=[
            pl.BlockSpec((tm, d), lambda i: (i, 0)),
            pl.BlockSpec((1, d), lambda i: (0, 0)),
            pl.BlockSpec((1, d), lambda i: (0, 0)),
            pl.BlockSpec((d, QKV_DIM), lambda i: (0, 0)),
        ],
        out_specs=[
            pl.BlockSpec((tm // dil, dil * QKV_DIM), lambda i: (i, 0)),
            pl.BlockSpec((tm, 2 * GROUP_DIM), lambda i: (i, 0)),
        ],
        out_shape=[
            jax.ShapeDtypeStruct((m // dil, dil * QKV_DIM), BF16),
            jax.ShapeDtypeStruct((m, 2 * GROUP_DIM), F32),
        ],
        scratch_shapes=[pltpu.VMEM((GROUP_HEADS, tm, B_HEAD_DIM), F32)],
        compiler_params=_cparams("parallel"),
        name="dilated_proj",
    )(x, gq, gkv, w)


def _dil_prompt_kernel(q_ref, kc_ref, kp_ref, vc_ref, vp_ref, bias_ref, o_ref, lse_ref):
    first = pl.program_id(2) == 0
    for h in range(GROUP_HEADS):
        hs = slice(h * B_HEAD_DIM, (h + 1) * B_HEAD_DIM)
        q = q_ref[:, hs]
        k = jnp.concatenate([kp_ref[:, hs], kc_ref[:, hs]], axis=0)
        v = jnp.concatenate([vp_ref[:, hs], vc_ref[:, hs]], axis=0)
        s = _dot_nt(q, k) * B_SCALE + bias_ref[h]
        col = lax.broadcasted_iota(jnp.int32, s.shape, 1)
        s = jnp.where(jnp.logical_and(first, col < DIL_TILE), NEG_INF, s)
        m = jnp.max(s, axis=-1, keepdims=True)
        p = jnp.exp(s - m)
        l = jnp.sum(p, axis=-1, keepdims=True)
        o_ref[:, hs] = _dot(p.astype(BF16), v) / l
        lse_ref[:, hs] = jnp.broadcast_to(m + jnp.log(l), (DIL_TILE, B_HEAD_DIM))


def _dil_prompt(qkv, bias, dil, batch, seq):
    sd = seq // dil
    nq = sd // DIL_TILE
    blk = (DIL_TILE, GROUP_DIM)
    cur = lambda b, i: b * nq + i
    prev = lambda b, i: b * nq + jnp.maximum(i - 1, 0)
    return pl.pallas_call(
        _dil_prompt_kernel,
        grid=(batch, dil, nq),
        in_specs=[
            pl.BlockSpec(blk, lambda b, r, i: (cur(b, i), 3 * r)),
            pl.BlockSpec(blk, lambda b, r, i: (cur(b, i), 3 * r + 1)),
            pl.BlockSpec(blk, lambda b, r, i: (prev(b, i), 3 * r + 1)),
            pl.BlockSpec(blk, lambda b, r, i: (cur(b, i), 3 * r + 2)),
            pl.BlockSpec(blk, lambda b, r, i: (prev(b, i), 3 * r + 2)),
            pl.BlockSpec((GROUP_HEADS, DIL_TILE, 2 * DIL_TILE), lambda b, r, i: (0, 0, 0)),
        ],
        out_specs=[
            pl.BlockSpec(blk, lambda b, r, i: (cur(b, i), r)),
            pl.BlockSpec(blk, lambda b, r, i: (cur(b, i), r)),
        ],
        out_shape=[
            jax.ShapeDtypeStruct((batch * sd, dil * GROUP_DIM), F32),
            jax.ShapeDtypeStruct((batch * sd, dil * GROUP_DIM), F32),
        ],
        compiler_params=_cparams("parallel", "parallel", "arbitrary"),
        name="dilated_prompt",
    )(qkv, qkv, qkv, qkv, qkv, bias)


def _dil_sample_kernel(q_ref, kn_ref, vn_ref, buf_ref, bias_ref, o_ref, lse_ref, *, dil):
    row = lax.broadcasted_iota(jnp.int32, (SUBLANES, B_HEAD_DIM), 0)
    row_wide = lax.broadcasted_iota(jnp.int32, (SUBLANES, GROUP_DIM), 0)
    own = row_wide == lax.broadcasted_iota(jnp.int32, (SUBLANES, GROUP_DIM), 1) // B_HEAD_DIM
    bias_keys = bias_ref[:, :DIL_KEYS]
    bias_new = bias_ref[:, DIL_KEYS:DIL_KEYS + 1]

    def cached(b, r):
        if dil == 1:
            return buf_ref[b, pl.ds(r, DIL_KEYS, stride=BUF_ROWS), :]
        return buf_ref[b, :, r, :]

    for b in range(DIL_SAMPLE_BLOCK):
        qb = q_ref[b:b + 1, :]
        s = jnp.zeros((SUBLANES, DIL_KEYS), F32)
        for h in range(GROUP_HEADS):
            qh = jnp.broadcast_to(qb[:, h * B_HEAD_DIM:(h + 1) * B_HEAD_DIM], (SUBLANES, B_HEAD_DIM))
            sh = _dot_nt(qh.astype(BF16), cached(b, 2 * h).astype(BF16))
            s = jnp.where(row == h, sh, s)
        s = s * B_SCALE + bias_keys
        kn = kn_ref[b:b + 1, :].astype(BF16).astype(F32)
        vn = vn_ref[b:b + 1, :].astype(BF16).astype(F32)
        q_own = jnp.where(own, qb, 0.0).astype(BF16).astype(F32)
        s_new = jnp.sum(q_own * kn, axis=-1, keepdims=True) * B_SCALE + bias_new
        m = jnp.maximum(jnp.max(s, axis=-1, keepdims=True), s_new)
        p = jnp.exp(s - m)
        p_new = jnp.exp(s_new - m)
        l = jnp.sum(p, axis=-1, keepdims=True) + p_new
        p16 = p.astype(BF16)
        pn = p_new.astype(BF16).astype(F32)
        lse = m + jnp.log(l)
        for h in range(GROUP_HEADS):
            hs = slice(h * B_HEAD_DIM, (h + 1) * B_HEAD_DIM)
            oh = (_dot(p16, cached(b, 2 * h + 1).astype(BF16)) + pn * vn[:, hs]) / l
            o_ref[b:b + 1, hs] = jnp.sum(jnp.where(row == h, oh, 0.0), axis=0, keepdims=True)
            lse_ref[b:b + 1, hs] = jnp.sum(jnp.where(row == h, lse, 0.0), axis=0, keepdims=True)


def _dil_sample(q_s, kv_s, buf, bias, g, dil):
    db, length = buf.shape[:2]
    bb = DIL_SAMPLE_BLOCK
    rows = jnp.transpose(buf, (0, 1, 3, 2, 4))
    if dil == 1:
        rows = rows.reshape(db, length * BUF_ROWS, B_HEAD_DIM)
        buf_spec = pl.BlockSpec((bb, length * BUF_ROWS, B_HEAD_DIM), lambda i: (i, 0, 0))
    else:
        rows = rows.reshape(db, length // dil, dil * BUF_ROWS, B_HEAD_DIM)
        buf_spec = pl.BlockSpec((bb, DIL_KEYS, BUF_ROWS_PAD, B_HEAD_DIM), lambda i: (i, 0, 0, 0))
    return pl.pallas_call(
        functools.partial(_dil_sample_kernel, dil=dil),
        grid=(db // bb,),
        in_specs=[
            pl.BlockSpec((bb, GROUP_DIM), lambda i: (i, g)),
            pl.BlockSpec((bb, GROUP_DIM), lambda i: (i, g)),
            pl.BlockSpec((bb, GROUP_DIM), lambda i: (i, N_GROUPS + g)),
            buf_spec,
            pl.BlockSpec((SUBLANES, 2 * DIL_KEYS), lambda i: (0, 0)),
        ],
        out_specs=[
            pl.BlockSpec((bb, GROUP_DIM), lambda i: (i, 0)),
            pl.BlockSpec((bb, GROUP_DIM), lambda i: (i, 0)),
        ],
        out_shape=[
            jax.ShapeDtypeStruct((db, GROUP_DIM), F32),
            jax.ShapeDtypeStruct((db, GROUP_DIM), F32),
        ],
        compiler_params=_cparams("parallel"),
        name="dilated_sample",
    )(q_s, kv_s, kv_s, rows, bias)


def _merge_out_kernel(o0, o1, o2, l0, l1, l2, w_ref, x_ref, g_ref, out_ref, merged_ref, *scratch, dils):
    srcs = ((o0, l0), (o1, l1), (o2, l2))
    tm = x_ref.shape[0]
    ordered = []
    si = 0
    for (o_src, l_src), dil in zip(srcs, dils):
        if dil == 1:
            ordered.append(None)
            continue
        o_dst, l_dst = scratch[si], scratch[si + 1]
        si += 2
        n = tm // dil
        for r in range(dil):
            for h in range(GROUP_HEADS):
                lo = r * GROUP_DIM + h * B_HEAD_DIM
                o_dst[h, pl.ds(r, n, stride=dil), :] = o_src[:, lo:lo + B_HEAD_DIM]
                l_dst[h, pl.ds(r, n, stride=dil), :] = l_src[:, lo:lo + B_HEAD_DIM]
        ordered.append((o_dst, l_dst))
    for h in range(GROUP_HEADS):
        hs = slice(h * B_HEAD_DIM, (h + 1) * B_HEAD_DIM)
        os_, ls_ = [], []
        for (o_src, l_src), dst in zip(srcs, ordered):
            os_.append(o_src[:, hs] if dst is None else dst[0][h])
            ls_.append(l_src[:, hs] if dst is None else dst[1][h])
        mx = jnp.maximum(jnp.maximum(ls_[0], ls_[1]), ls_[2])
        e0, e1, e2 = jnp.exp(ls_[0] - mx), jnp.exp(ls_[1] - mx), jnp.exp(ls_[2] - mx)
        merged = (e0 * os_[0] + e1 * os_[1] + e2 * os_[2]) / (e0 + e1 + e2)
        merged_ref[:, hs] = merged.astype(BF16)
    y = _dot(merged_ref[...], w_ref[...])
    out_ref[...] = x_ref[...] + _rms(y, g_ref[...])


def _merge_out(outs, lses, dils, w, x, g, tm):
    m, d = x.shape
    parts = [pl.BlockSpec((tm // dil, dil * GROUP_DIM), lambda i: (i, 0)) for dil in dils]
    scratch = [pltpu.VMEM((tm, GROUP_DIM), BF16)]
    for dil in dils:
        if dil > 1:
            scratch += [pltpu.VMEM((GROUP_HEADS, tm, B_HEAD_DIM), F32)] * 2
    return pl.pallas_call(
        functools.partial(_merge_out_kernel, dils=tuple(dils)),
        grid=(m // tm,),
        in_specs=parts + parts + [
            pl.BlockSpec((GROUP_DIM, d), lambda i: (0, 0)),
            pl.BlockSpec((tm, d), lambda i: (i, 0)),
            pl.BlockSpec((1, d), lambda i: (0, 0)),
        ],
        out_specs=pl.BlockSpec((tm, d), lambda i: (i, 0)),
        out_shape=jax.ShapeDtypeStruct((m, d), F32),
        scratch_shapes=scratch,
        compiler_params=_cparams("parallel"),
        name="dilated_merge_out",
    )(*outs, *lses, w, x, g)


def _t5_bucket(dist):
    max_exact = T5_BUCKETS // 2
    d = np.maximum(dist, 1).astype(np.float32)
    large = max_exact + (np.log(d / max_exact) / math.log(T5_MAX_DIST / max_exact)
                         * (T5_BUCKETS - max_exact)).astype(np.int32)
    large = np.minimum(large, T5_BUCKETS - 1)
    return np.where(dist < max_exact, dist, large).astype(np.int32)


def _rows_static(table, idx):
    pieces, start = [], 0
    for i in range(1, len(idx) + 1):
        if i == len(idx) or idx[i] != idx[start]:
            row = table[int(idx[start]):int(idx[start]) + 1]
            pieces.append(jnp.broadcast_to(row, (i - start,) + table.shape[1:]))
            start = i
    return jnp.concatenate(pieces, axis=0)


def _group_bias(t5_bias, g):
    window, dil = DIL_GROUPS[g]
    buckets = _t5_bucket(dil * np.arange(window // dil + 1))
    return _rows_static(t5_bias[:, g * GROUP_HEADS:(g + 1) * GROUP_HEADS].astype(F32), buckets)


def _prompt_bias(bias):
    n = 2 * DIL_TILE
    heads = bias.shape[1]
    period = jnp.concatenate([bias[::-1], jnp.full((n - DIL_KEYS, heads), NEG_INF, F32)], axis=0)
    band = jnp.tile(period, (DIL_TILE, 1))[:DIL_TILE * n].reshape(DIL_TILE, n, heads)
    return jnp.transpose(band, (2, 0, 1))


def _sample_bias(bias):
    keys = bias[1:][::-1].T
    own = jnp.broadcast_to(bias[0][:, None], (GROUP_HEADS, DIL_KEYS))
    both = jnp.concatenate([keys, own], axis=1)
    return jnp.pad(both, ((0, SUBLANES - GROUP_HEADS), (0, 0)))


def _rope_tables(pos):
    half = ROPE_DIM // 2
    inv = ROPE_THETA ** (-jnp.arange(half, dtype=F32) / half)
    ang = pos.astype(F32)[:, None] * inv
    cos, sin = jnp.cos(ang), jnp.sin(ang)
    zero = jnp.zeros((pos.shape[0], LANES - ROPE_DIM), F32)
    return (jnp.concatenate([cos, cos, zero], axis=1), jnp.concatenate([-sin, sin, zero], axis=1))


def _tile(n, pref):
    return pref if n % pref == 0 else n


def kernel(x_prompt, x_sample, cache_mla, page_table, cache_b_win0, cache_b_win1, cache_b_win2,
           norm_gains, w_mla_in, g_q_a, g_kv_a, w_uq, w_uk, w_uv, w_mla_o,
           g_kv_b, w_kv_b, w_q_b, w_b_o, t5_bias, w_ffn_in, w_ffn_out):
    batch, seq, d = x_prompt.shape
    db = x_sample.shape[0]
    assert x_sample.shape[1] == 1 and DEPTH == 2 and N_A_LAYERS == 1
    win_bufs = (cache_b_win0, cache_b_win1, cache_b_win2)
    for gi, (window, dil) in enumerate(DIL_GROUPS):
        assert win_bufs[gi].shape[1] == window and window // dil == DIL_KEYS
        assert seq % (dil * DIL_TILE) == 0
    mp = batch * seq
    xp = x_prompt.reshape(mp, d)
    xs = x_sample.reshape(db, d)
    gains = norm_gains.reshape(DEPTH, 4, 1, d)
    tm_p = _tile(mp, 1024)
    tm_h = _tile(mp, 512)

    rope_cols = jnp.pad(w_mla_in[0][:, Q_LORA + KV_LORA:], ((0, 0), (0, LANES - ROPE_DIM)))
    w_in_pad = jnp.concatenate([w_mla_in[0][:, :Q_LORA + KV_LORA], rope_cols], axis=1).astype(BF16)
    w_uq_pad = jnp.pad(w_uq[0].reshape(Q_LORA, MLA_HEADS, NOPE_DIM + ROPE_DIM),
                       ((0, 0), (0, 0), (0, HEAD_PAD - NOPE_DIM - ROPE_DIM)))
    w_uq_pad = w_uq_pad.reshape(Q_LORA, MLA_HEADS * HEAD_PAD).astype(BF16)
    w_k_lat = jnp.pad(w_uk[0], ((0, 0), (0, 0), (0, HEAD_PAD - NOPE_DIM)))
    rope_copy = jnp.pad(jnp.eye(ROPE_DIM, dtype=F32), ((0, 0), (NOPE_DIM, HEAD_PAD - NOPE_DIM - ROPE_DIM)))
    rope_copy = jnp.broadcast_to(rope_copy[:, None, :], (ROPE_DIM, MLA_HEADS, HEAD_PAD))
    w_k = jnp.concatenate([w_k_lat, rope_copy], axis=0).reshape(ROW_DIM, MLA_HEADS * HEAD_PAD)
    w_v = jnp.pad(w_uv[0].reshape(KV_LORA, MLA_HEADS * V_DIM), ((0, ROPE_DIM), (0, 0)))
    w_kv = jnp.concatenate([w_k, w_v], axis=1).astype(BF16)
    w_uk_t = jnp.transpose(w_uk[0], (1, 2, 0)).astype(BF16)
    w_uv_t = jnp.transpose(w_uv[0], (1, 0, 2)).astype(BF16)
    w_o = w_mla_o[0].astype(BF16)
    w_kvb = w_kv_b.astype(BF16)
    w_qb = w_q_b[0].astype(BF16)
    w_bo = w_b_o[0].astype(BF16)
    w_f_in = w_ffn_in.astype(BF16)
    w_f_out = w_ffn_out.astype(BF16)
    f = w_ffn_out.shape[1]
    tf = _tile(f, 512)

    c_p, s_p = _rope_tables(jnp.tile(jnp.arange(seq), batch))
    c_s, s_s = _rope_tables(jnp.full((db,), PAST_LEN, jnp.int32))

    g = gains[0]
    cq_p, rows_p = _mla_in(xp, g[0], w_in_pad, g_q_a[0:1], g_kv_a[0:1], c_p, s_p, tm_h)
    cq_s, rows_s = _mla_in(xs, g[0], w_in_pad, g_q_a[0:1], g_kv_a[0:1], c_s, s_s, db)
    q_p = _q_up(cq_p, w_uq_pad, c_p, s_p, tm_p, 1024)
    q_s = _q_up(cq_s, w_uq_pad, c_s, s_s, db, 1024)

    kv_p = _cast_mm(rows_p, w_kv, BF16, tm_p, 1024)
    attn_p = _mla_prompt_attention(q_p, kv_p, batch, seq, _tile(seq, 512))
    xp = _mm_norm_res(attn_p, w_o, xp, g[1], tm_p, 512)

    q_lat = _q_absorb(q_s, w_uk_t)
    q_rope_s = q_s.reshape(db, MLA_HEADS, HEAD_PAD)[:, :, NOPE_DIM:NOPE_DIM + ROPE_DIM]
    q_cat = jnp.concatenate([jnp.transpose(q_lat, (1, 0, 2)), q_rope_s], axis=-1)
    cache_t = jnp.swapaxes(cache_mla[0], 1, 2)
    o_lat = _mla_decode(q_cat, rows_s, cache_t, page_table)
    attn_s = _v_up(jnp.transpose(o_lat, (1, 0, 2)).astype(BF16), w_uv_t)
    xs = _mm_norm_res(attn_s, w_o, xs, g[1], db, 512)

    xp = _ffn(xp, g[2], g[3], w_f_in[0], w_f_out[0], tm_h, tf)
    xs = _ffn(xs, g[2], g[3], w_f_in[0], w_f_out[0], db, tf)

    g = gains[1]
    g_kvb = g_kv_b.reshape(1, d)
    kvb_s = _norm_mm(xs, g_kvb, w_kvb, F32, db, 512)
    qb_s = _norm_mm(xs, g[0], w_qb, F32, db, GROUP_DIM)
    outs_p, lses_p, outs_s, lses_s, kv_groups = [], [], [], [], []
    dils = [dil for _, dil in DIL_GROUPS]
    biases = [_group_bias(t5_bias, gi) for gi in range(N_GROUPS)]
    band_all = _prompt_bias(jnp.concatenate(biases, axis=1))
    for gi, (window, dil) in enumerate(DIL_GROUPS):
        cols = slice(gi * GROUP_DIM, (gi + 1) * GROUP_DIM)
        w_g = jnp.concatenate([w_qb[:, cols], w_kvb[:, cols],
                               w_kvb[:, N_GROUPS * GROUP_DIM:][:, cols]], axis=1)
        qkv, kv_g = _dil_proj(xp, g[0], g_kvb, w_g, dil, tm_h)
        kv_groups.append(kv_g)
        bias = biases[gi]
        o, lse = _dil_prompt(qkv, band_all[gi * GROUP_HEADS:(gi + 1) * GROUP_HEADS], dil, batch, seq)
        outs_p.append(o)
        lses_p.append(lse)
        o, lse = _dil_sample(qb_s, kvb_s, win_bufs[gi], _sample_bias(bias), gi, dil)
        outs_s.append(o)
        lses_s.append(lse)
    xp = _merge_out(outs_p, lses_p, dils, w_bo, xp, g[1], _tile(mp, 256))
    xs = _merge_out(outs_s, lses_s, [1] * N_GROUPS, w_bo, xs, g[1], db)

    xp = _ffn(xp, g[2], g[3], w_f_in[1], w_f_out[1], tm_h, tf)
    xs = _ffn(xs, g[2], g[3], w_f_in[1], w_f_out[1], db, tf)

    kv6_s = kvb_s.reshape(db, 1, 2, N_GROUPS, GROUP_HEADS, B_HEAD_DIM)
    wins = []
    for gi, (window, _) in enumerate(DIL_GROUPS):
        kv5 = kv_groups[gi].reshape(batch, seq, 2, GROUP_HEADS, B_HEAD_DIM)
        wins.append(kv5[:, seq - min(window, seq):])
        wins.append(kv6_s[:, :, :, gi])
    return (xp.reshape(batch, seq, d), xs.reshape(db, 1, d),
            rows_p.reshape(1, batch, seq, ROW_DIM), rows_s.reshape(1, db, 1, ROW_DIM), *wins)
```

```python
import functools
import math

import numpy as np
import jax
import jax.numpy as jnp
from jax import lax
from jax.experimental import pallas as pl
from jax.experimental.pallas import tpu as pltpu

D_MODEL = 2048
DEPTH = 2
PAST_LEN = 16384
PAGE_SIZE = 128
N_A_LAYERS = DEPTH // 2
MLA_HEADS = 16
Q_LORA = 512
KV_LORA = 512
NOPE_DIM = 128
ROPE_DIM = 64
V_DIM = 128
ROPE_THETA = 10000.0
MLA_SCALE = (NOPE_DIM + ROPE_DIM) ** -0.5
DIL_GROUPS = ((128, 1), (512, 4), (2048, 16))
N_GROUPS = len(DIL_GROUPS)
GROUP_HEADS = 6
B_HEAD_DIM = 128
B_HEADS = N_GROUPS * GROUP_HEADS
B_SCALE = B_HEAD_DIM ** -0.5
T5_BUCKETS = 32
T5_MAX_DIST = 2048
EPS = 1e-6
NEG_INF = -1e30

F32 = jnp.float32
BF16 = jnp.bfloat16

VMEM_LIMIT_BYTES = 56 * 1024 * 1024
LANES = 128
SUBLANES = 8
HEAD_PAD = 2 * LANES
ROW_DIM = KV_LORA + ROPE_DIM
GROUP_DIM = GROUP_HEADS * B_HEAD_DIM
QKV_DIM = 3 * GROUP_DIM
DIL_KEYS = 128
DIL_TILE = 128
FLASH_HEADS = 4
DEC_PAGES = 32
DEC_SLOTS = 3
DIL_SAMPLE_BLOCK = 8
BUF_ROWS = 2 * GROUP_HEADS
BUF_ROWS_PAD = 16


def _cparams(*semantics):
    return pltpu.CompilerParams(dimension_semantics=semantics, vmem_limit_bytes=VMEM_LIMIT_BYTES)


def _rms(x, g):
    return x * lax.rsqrt(jnp.mean(x * x, axis=-1, keepdims=True) + EPS) * g


def _dot(a, b):
    return jnp.dot(a, b, preferred_element_type=F32)


def _dot_nt(a, b):
    return lax.dot_general(a, b, (((1,), (1,)), ((), ())), preferred_element_type=F32)


def _dot_tn(a, b):
    return lax.dot_general(a, b, (((0,), (0,)), ((), ())), preferred_element_type=F32)


def _rope128(x, c, s):
    lane = lax.broadcasted_iota(jnp.int32, x.shape, 1)
    half = ROPE_DIM // 2
    swapped = jnp.where(lane < half, pltpu.roll(x, LANES - half, 1), pltpu.roll(x, half, 1))
    return x * c + swapped * s


def _mla_in_kernel(x_ref, g_ref, w_ref, gq_ref, gkv_ref, c_ref, s_ref, cq_ref, rows_ref):
    xn = _rms(x_ref[...], g_ref[...]).astype(BF16)
    a = _dot(xn, w_ref[...])
    cq_ref[...] = _rms(a[:, :Q_LORA], gq_ref[...]).astype(BF16)
    rows_ref[:, :KV_LORA] = _rms(a[:, Q_LORA:Q_LORA + KV_LORA], gkv_ref[...])
    kr = _rope128(a[:, Q_LORA + KV_LORA:], c_ref[...], s_ref[...])
    rows_ref[:, KV_LORA:] = kr[:, :ROPE_DIM]


def _mla_in(x, g, w_pad, gq, gkv, c_tab, s_tab, tm):
    m, d = x.shape
    n = w_pad.shape[1]
    return pl.pallas_call(
        _mla_in_kernel,
        grid=(m // tm,),
        in_specs=[
            pl.BlockSpec((tm, d), lambda i: (i, 0)),
            pl.BlockSpec((1, d), lambda i: (0, 0)),
            pl.BlockSpec((d, n), lambda i: (0, 0)),
            pl.BlockSpec((1, Q_LORA), lambda i: (0, 0)),
            pl.BlockSpec((1, KV_LORA), lambda i: (0, 0)),
            pl.BlockSpec((tm, LANES), lambda i: (i, 0)),
            pl.BlockSpec((tm, LANES), lambda i: (i, 0)),
        ],
        out_specs=[
            pl.BlockSpec((tm, Q_LORA), lambda i: (i, 0)),
            pl.BlockSpec((tm, ROW_DIM), lambda i: (i, 0)),
        ],
        out_shape=[
            jax.ShapeDtypeStruct((m, Q_LORA), BF16),
            jax.ShapeDtypeStruct((m, ROW_DIM), F32),
        ],
        compiler_params=_cparams("parallel"),
        name="mla_in",
    )(x, g, w_pad, gq, gkv, c_tab, s_tab)


def _q_up_kernel(cq_ref, w_ref, c_ref, s_ref, o_ref, *, heads_per_tile, out_scale):
    a = _dot(cq_ref[...], w_ref[...])
    if out_scale != 1.0:
        a = a * out_scale
    c = c_ref[...]
    s = s_ref[...]
    for h in range(heads_per_tile):
        lo = h * HEAD_PAD
        o_ref[:, lo:lo + LANES] = a[:, lo:lo + LANES].astype(BF16)
        o_ref[:, lo + LANES:lo + HEAD_PAD] = _rope128(a[:, lo + LANES:lo + HEAD_PAD], c, s).astype(BF16)


def _q_up(cq, w_pad, c_tab, s_tab, tm, tn, out_scale=1.0):
    m, k = cq.shape
    n = w_pad.shape[1]
    return pl.pallas_call(
        functools.partial(_q_up_kernel, heads_per_tile=tn // HEAD_PAD, out_scale=out_scale),
        grid=(m // tm, n // tn),
        in_specs=[
            pl.BlockSpec((tm, k), lambda i, j: (i, 0)),
            pl.BlockSpec((k, tn), lambda i, j: (0, j)),
            pl.BlockSpec((tm, LANES), lambda i, j: (i, 0)),
            pl.BlockSpec((tm, LANES), lambda i, j: (i, 0)),
        ],
        out_specs=pl.BlockSpec((tm, tn), lambda i, j: (i, j)),
        out_shape=jax.ShapeDtypeStruct((m, n), BF16),
        compiler_params=_cparams("parallel", "parallel"),
        name="mla_q_up",
    )(cq, w_pad, c_tab, s_tab)


def _cast_mm_kernel(x_ref, w_ref, o_ref):
    o_ref[...] = _dot(x_ref[...].astype(BF16), w_ref[...]).astype(o_ref.dtype)


def _cast_mm(x, w, out_dtype, tm, tn):
    m, k = x.shape
    n = w.shape[1]
    return pl.pallas_call(
        _cast_mm_kernel,
        grid=(m // tm, n // tn),
        in_specs=[
            pl.BlockSpec((tm, k), lambda i, j: (i, 0)),
            pl.BlockSpec((k, tn), lambda i, j: (0, j)),
        ],
        out_specs=pl.BlockSpec((tm, tn), lambda i, j: (i, j)),
        out_shape=jax.ShapeDtypeStruct((m, n), out_dtype),
        compiler_params=_cparams("parallel", "parallel"),
        name="cast_mm",
    )(x, w)


def _norm_mm_kernel(x_ref, g_ref, w_ref, o_ref, xn_ref):
    @pl.when(pl.program_id(1) == 0)
    def _():
        xn_ref[...] = _rms(x_ref[...], g_ref[...]).astype(BF16)

    o_ref[...] = _dot(xn_ref[...], w_ref[...]).astype(o_ref.dtype)


def _norm_mm(x, g, w, out_dtype, tm, tn):
    m, d = x.shape
    n = w.shape[1]
    return pl.pallas_call(
        _norm_mm_kernel,
        grid=(m // tm, n // tn),
        in_specs=[
            pl.BlockSpec((tm, d), lambda i, j: (i, 0)),
            pl.BlockSpec((1, d), lambda i, j: (0, 0)),
            pl.BlockSpec((d, tn), lambda i, j: (0, j)),
        ],
        out_specs=pl.BlockSpec((tm, tn), lambda i, j: (i, j)),
        out_shape=jax.ShapeDtypeStruct((m, n), out_dtype),
        scratch_shapes=[pltpu.VMEM((tm, d), BF16)],
        compiler_params=_cparams("parallel", "arbitrary"),
        name="norm_mm",
    )(x, g, w)


def _mm_norm_res_kernel(a_ref, w_ref, x_ref, g_ref, o_ref, *, nk):
    k = pl.program_id(1)

    @pl.when(k == 0)
    def _():
        o_ref[...] = jnp.zeros(o_ref.shape, F32)

    o_ref[...] += _dot(a_ref[...], w_ref[...])

    @pl.when(k == nk - 1)
    def _():
        o_ref[...] = x_ref[...] + _rms(o_ref[...], g_ref[...])


def _mm_norm_res(a, w, x, g, tm, tk):
    m, kdim = a.shape
    d = w.shape[1]
    nk = kdim // tk
    return pl.pallas_call(
        functools.partial(_mm_norm_res_kernel, nk=nk),
        grid=(m // tm, nk),
        in_specs=[
            pl.BlockSpec((tm, tk), lambda i, k: (i, k)),
            pl.BlockSpec((tk, d), lambda i, k: (k, 0)),
            pl.BlockSpec((tm, d), lambda i, k: (i, 0)),
            pl.BlockSpec((1, d), lambda i, k: (0, 0)),
        ],
        out_specs=pl.BlockSpec((tm, d), lambda i, k: (i, 0)),
        out_shape=jax.ShapeDtypeStruct((m, d), F32),
        compiler_params=_cparams("parallel", "arbitrary"),
        name="mm_norm_res",
    )(a, w, x, g)


def _ffn_kernel(x_ref, g_in_ref, g_out_ref, wg_ref, wu_ref, wo_ref, o_ref, xn_ref, *, nf):
    j = pl.program_id(1)

    @pl.when(j == 0)
    def _():
        xn_ref[...] = _rms(x_ref[...], g_in_ref[...]).astype(BF16)
        o_ref[...] = jnp.zeros(o_ref.shape, F32)

    xn = xn_ref[...]
    tf = wo_ref.shape[0]
    half = tf // 2 if tf % (2 * LANES) == 0 else tf
    for lo in range(0, tf, half):
        gate = _dot(xn, wg_ref[:, lo:lo + half])
        up = _dot(xn, wu_ref[:, lo:lo + half])
        act = (gate * jax.nn.sigmoid(gate) * up).astype(BF16)
        o_ref[...] += _dot(act, wo_ref[lo:lo + half, :])

    @pl.when(j == nf - 1)
    def _():
        o_ref[...] = x_ref[...] + _rms(o_ref[...], g_out_ref[...])


def _ffn(x, g_in, g_out, w_in, w_out, tm, tf):
    m, d = x.shape
    f = w_out.shape[0]
    nf = f // tf
    return pl.pallas_call(
        functools.partial(_ffn_kernel, nf=nf),
        grid=(m // tm, nf),
        in_specs=[
            pl.BlockSpec((tm, d), lambda i, j: (i, 0)),
            pl.BlockSpec((1, d), lambda i, j: (0, 0)),
            pl.BlockSpec((1, d), lambda i, j: (0, 0)),
            pl.BlockSpec((d, tf), lambda i, j: (0, j)),
            pl.BlockSpec((d, tf), lambda i, j: (0, j + nf)),
            pl.BlockSpec((tf, d), lambda i, j: (j, 0)),
        ],
        out_specs=pl.BlockSpec((tm, d), lambda i, j: (i, 0)),
        out_shape=jax.ShapeDtypeStruct((m, d), F32),
        scratch_shapes=[pltpu.VMEM((tm, d), BF16)],
        compiler_params=_cparams("parallel", "arbitrary"),
        name="ffn",
    )(x, g_in, g_out, w_in, w_in, w_out)


def _flash_kernel(q_ref, k_ref, v_ref, o_ref, m_ref, l_ref, acc_ref, *, tq):
    qi = pl.program_id(2)
    heads = FLASH_HEADS
    qts = [q_ref[:, h * HEAD_PAD:(h + 1) * HEAD_PAD].astype(F32).T.astype(BF16) for h in range(heads)]
    m_ref[...] = jnp.full(m_ref.shape, -jnp.inf, F32)
    l_ref[...] = jnp.zeros(l_ref.shape, F32)
    acc_ref[...] = jnp.zeros(acc_ref.shape, F32)

    def step(kc, diagonal):
        start = pl.multiple_of(kc * tq, tq)
        scores = [_dot(k_ref[pl.ds(start, tq), h * HEAD_PAD:(h + 1) * HEAD_PAD], qts[h]) for h in range(heads)]
        for h in range(heads):
            v = v_ref[pl.ds(start, tq), h * V_DIM:(h + 1) * V_DIM]
            s = scores[h]
            if diagonal:
                key = lax.broadcasted_iota(jnp.int32, s.shape, 0)
                query = lax.broadcasted_iota(jnp.int32, s.shape, 1)
                s = jnp.where(key <= query, s, NEG_INF)
            m_prev = m_ref[h]
            m_new = jnp.maximum(m_prev, jnp.max(s, axis=0, keepdims=True))
            alpha = jnp.exp2(m_prev - m_new)
            p = jnp.exp2(s - m_new)
            l_ref[h] = alpha * l_ref[h] + jnp.sum(p, axis=0, keepdims=True)
            acc_ref[h] = alpha * acc_ref[h] + _dot_tn(v, p.astype(BF16))
            m_ref[h] = m_new

    def body(kc, carry):
        step(kc, False)
        return carry

    lax.fori_loop(0, qi, body, 0)
    step(qi, True)
    for h in range(heads):
        o_ref[:, h * V_DIM:(h + 1) * V_DIM] = (acc_ref[h] / l_ref[h]).T.astype(o_ref.dtype)


def _mla_prompt_attention(q, kv, batch, seq, tq):
    nq = seq // tq
    hb = FLASH_HEADS
    v_block0 = MLA_HEADS * HEAD_PAD // (hb * V_DIM)
    return pl.pallas_call(
        functools.partial(_flash_kernel, tq=tq),
        grid=(batch, MLA_HEADS // hb, nq),
        in_specs=[
            pl.BlockSpec((tq, hb * HEAD_PAD), lambda b, h, i: (b * nq + i, h)),
            pl.BlockSpec((seq, hb * HEAD_PAD), lambda b, h, i: (b, h)),
            pl.BlockSpec((seq, hb * V_DIM), lambda b, h, i: (b, v_block0 + h)),
        ],
        out_specs=pl.BlockSpec((tq, hb * V_DIM), lambda b, h, i: (b * nq + i, h)),
        out_shape=jax.ShapeDtypeStruct((batch * seq, MLA_HEADS * V_DIM), BF16),
        scratch_shapes=[
            pltpu.VMEM((hb, 1, tq), F32),
            pltpu.VMEM((hb, 1, tq), F32),
            pltpu.VMEM((hb, V_DIM, tq), F32),
        ],
        compiler_params=_cparams("parallel", "parallel", "arbitrary"),
        name="mla_prompt_attention",
    )(q, kv, kv)


def _head_mm_kernel(a_ref, w_ref, o_ref):
    o_ref[...] = _dot(a_ref[...], w_ref[...]).astype(o_ref.dtype)


def _q_absorb(q_s, w_uk_t):
    db = q_s.shape[0]
    return pl.pallas_call(
        _head_mm_kernel,
        grid=(MLA_HEADS,),
        in_specs=[
            pl.BlockSpec((db, NOPE_DIM), lambda h: (0, 2 * h)),
            pl.BlockSpec((None, NOPE_DIM, KV_LORA), lambda h: (h, 0, 0)),
        ],
        out_specs=pl.BlockSpec((None, db, KV_LORA), lambda h: (h, 0, 0)),
        out_shape=jax.ShapeDtypeStruct((MLA_HEADS, db, KV_LORA), BF16),
        compiler_params=_cparams("parallel"),
        name="mla_q_absorb",
    )(q_s, w_uk_t)


def _v_up(o_lat_t, w_uv_t):
    db = o_lat_t.shape[1]
    return pl.pallas_call(
        _head_mm_kernel,
        grid=(MLA_HEADS,),
        in_specs=[
            pl.BlockSpec((None, db, KV_LORA), lambda h: (h, 0, 0)),
            pl.BlockSpec((None, KV_LORA, V_DIM), lambda h: (h, 0, 0)),
        ],
        out_specs=pl.BlockSpec((db, V_DIM), lambda h: (0, h)),
        out_shape=jax.ShapeDtypeStruct((db, MLA_HEADS * V_DIM), BF16),
        compiler_params=_cparams("parallel"),
        name="mla_v_up",
    )(o_lat_t, w_uv_t)


def _mla_decode_kernel(pt_ref, q_ref, new_ref, cache_ref, o_ref, pages_ref, kvt0_ref, kvt1_ref, s0_ref, s1_ref,
                       m_ref, l_ref, acc_ref, sem, *, n_pages, n_chunks, n_steps):
    kvt_refs = (kvt0_ref, kvt1_ref)
    s_refs = (s0_ref, s1_ref)
    t = pl.program_id(0)
    prev_chunk = (t + n_chunks - 1) % n_chunks

    def page_copy(step, p):
        chunk = jnp.minimum(step, n_steps - 2)
        slot = step % DEC_SLOTS
        return pltpu.make_async_copy(cache_ref.at[pt_ref[chunk * n_pages + p]], pages_ref.at[slot, p],
                                     sem.at[slot, p])

    def start_chunk(step):
        for p in range(n_pages):
            page_copy(step, p).start(priority=p % 2)

    @pl.when(t == 0)
    def _():
        for step in range(DEC_SLOTS - 1):
            start_chunk(step)
        kvt_refs[1][...] = jnp.zeros(kvt_refs[1].shape, BF16)
        s_refs[1][...] = jnp.zeros(s_refs[1].shape, F32)
        m_ref[...] = jnp.zeros(m_ref.shape, F32)
        l_ref[...] = jnp.zeros(l_ref.shape, F32)
        acc_ref[...] = jnp.zeros(acc_ref.shape, F32)

    @pl.when(jnp.logical_and(t > 0, prev_chunk == 0))
    def _():
        new = new_ref[...].astype(BF16).astype(F32)
        m_ref[...] = jnp.sum(q_ref[...].astype(F32) * new, axis=-1, keepdims=True) * MLA_SCALE
        l_ref[...] = jnp.ones(l_ref.shape, F32)
        acc_ref[...] = jnp.broadcast_to(new, acc_ref.shape)

    def work(cur, prev):
        kvt_cur, kvt_prev = kvt_refs[cur], kvt_refs[prev]
        slot = t % DEC_SLOTS
        start_chunk(t + DEC_SLOTS - 1)
        for p in range(n_pages):
            page_copy(t, p).wait()
            kvt_cur[:, p * PAGE_SIZE:(p + 1) * PAGE_SIZE] = pages_ref[slot, p].astype(BF16)
        s_refs[cur][...] = _dot(q_ref[...], kvt_cur[...]) * MLA_SCALE
        s = s_refs[prev][...]
        m_prev = m_ref[...]
        m_new = jnp.maximum(m_prev, jnp.max(s, axis=-1, keepdims=True))
        alpha = jnp.exp(m_prev - m_new)
        p = jnp.exp(s - m_new)
        l_ref[...] = alpha * l_ref[...] + jnp.sum(p, axis=-1, keepdims=True)
        acc_ref[...] = alpha * acc_ref[...] + _dot_nt(p.astype(BF16), kvt_prev[...])
        m_ref[...] = m_new

    @pl.when(t % 2 == 0)
    def _():
        work(0, 1)

    @pl.when(t % 2 == 1)
    def _():
        work(1, 0)

    @pl.when(jnp.logical_and(t > 0, prev_chunk == n_chunks - 1))
    def _():
        o_ref[...] = (acc_ref[...] / l_ref[...])[:, :KV_LORA]

    @pl.when(t == n_steps - 1)
    def _():
        for step in range(n_steps, n_steps + DEC_SLOTS - 1):
            for p in range(n_pages):
                page_copy(step, p).wait()


def _mla_decode(q_cat, rows_new, cache_t, page_table):
    db = q_cat.shape[0]
    n_tab = page_table.shape[1]
    n_pages = min(DEC_PAGES, n_tab // 2)
    n_chunks = n_tab // n_pages
    assert n_chunks >= 2 and n_chunks * n_pages == n_tab
    n_steps = db * n_chunks + 1
    width = n_pages * PAGE_SIZE
    chunk = lambda t: jnp.minimum(t, n_steps - 2)
    batch_of = lambda t: chunk(t) // n_chunks

    grid_spec = pltpu.PrefetchScalarGridSpec(
        num_scalar_prefetch=1,
        grid=(n_steps,),
        in_specs=[
            pl.BlockSpec((None, MLA_HEADS, ROW_DIM), lambda t, pt: (batch_of(t), 0, 0)),
            pl.BlockSpec((None, 1, ROW_DIM), lambda t, pt: (batch_of(t), 0, 0)),
            pl.BlockSpec(memory_space=pl.ANY),
        ],
        out_specs=pl.BlockSpec((None, MLA_HEADS, KV_LORA),
                               lambda t, pt: (jnp.maximum(t - 1, 0) // n_chunks, 0, 0)),
        scratch_shapes=[
            pltpu.VMEM((DEC_SLOTS, n_pages, ROW_DIM, PAGE_SIZE), F32),
            pltpu.VMEM((ROW_DIM, width), BF16),
            pltpu.VMEM((ROW_DIM, width), BF16),
            pltpu.VMEM((MLA_HEADS, width), F32),
            pltpu.VMEM((MLA_HEADS, width), F32),
            pltpu.VMEM((MLA_HEADS, 1), F32),
            pltpu.VMEM((MLA_HEADS, 1), F32),
            pltpu.VMEM((MLA_HEADS, ROW_DIM), F32),
            pltpu.SemaphoreType.DMA((DEC_SLOTS, n_pages)),
        ],
    )
    return pl.pallas_call(
        functools.partial(_mla_decode_kernel, n_pages=n_pages, n_chunks=n_chunks, n_steps=n_steps),
        grid_spec=grid_spec,
        out_shape=jax.ShapeDtypeStruct((db, MLA_HEADS, KV_LORA), F32),
        compiler_params=_cparams("arbitrary"),
        name="mla_decode",
    )(page_table.reshape(-1), q_cat, rows_new.reshape(db, 1, ROW_DIM), cache_t)


def _dil_proj_kernel(x_ref, gq_ref, gkv_ref, w_ref, qkv_ref, kv_ref, y_ref, *, dil):
    x = x_ref[...]
    xr = x * lax.rsqrt(jnp.mean(x * x, axis=-1, keepdims=True) + EPS)
    xq = (xr * gq_ref[...]).astype(BF16)
    xkv = (xr * gkv_ref[...]).astype(BF16)
    n = x.shape[0] // dil
    for j in range(3):
        y = _dot(xq if j == 0 else xkv, w_ref[:, j * GROUP_DIM:(j + 1) * GROUP_DIM])
        if j > 0:
            kv_ref[:, (j - 1) * GROUP_DIM:j * GROUP_DIM] = y
        if dil == 1:
            qkv_ref[:, j * GROUP_DIM:(j + 1) * GROUP_DIM] = y.astype(BF16)
            continue
        for h in range(GROUP_HEADS):
            y_ref[h] = y[:, h * B_HEAD_DIM:(h + 1) * B_HEAD_DIM]
        for r in range(dil):
            for h in range(GROUP_HEADS):
                lo = r * QKV_DIM + j * GROUP_DIM + h * B_HEAD_DIM
                qkv_ref[:, lo:lo + B_HEAD_DIM] = y_ref[h, pl.ds(r, n, stride=dil), :].astype(BF16)


def _dil_proj(x, gq, gkv, w, dil, tm):
    m, d = x.shape
    return pl.pallas_call(
        functools.partial(_dil_proj_kernel, dil=dil),
        grid=(m // tm,),
        in_specs=[
            pl.BlockSpec((tm, d), lambda i: (i, 0)),
            pl.BlockSpec((1, d), lambda i: (0, 0)),
            pl.BlockSpec((1, d), lambda i: (0, 0)),
            pl.BlockSpec((d, QKV_DIM), lambda i: (0, 0)),
        ],
        out_specs=[
            pl.BlockSpec((tm // dil, dil * QKV_DIM), lambda i: (i, 0)),
            pl.BlockSpec((tm, 2 * GROUP_DIM), lambda i: (i, 0)),
        ],
        out_shape=[
            jax.ShapeDtypeStruct((m // dil, dil * QKV_DIM), BF16),
            jax.ShapeDtypeStruct((m, 2 * GROUP_DIM), F32),
        ],
        scratch_shapes=[pltpu.VMEM((GROUP_HEADS, tm, B_HEAD_DIM), F32)],
        compiler_params=_cparams("parallel"),
        name="dilated_proj",
    )(x, gq, gkv, w)


def _dil_prompt_kernel(q_ref, kc_ref, kp_ref, vc_ref, vp_ref, bias_ref, o_ref, lse_ref):
    first = pl.program_id(2) == 0
    for h in range(GROUP_HEADS):
        hs = slice(h * B_HEAD_DIM, (h + 1) * B_HEAD_DIM)
        q = q_ref[:, hs]
        k = jnp.concatenate([kp_ref[:, hs], kc_ref[:, hs]], axis=0)
        v = jnp.concatenate([vp_ref[:, hs], vc_ref[:, hs]], axis=0)
        s = _dot_nt(q, k) * B_SCALE + bias_ref[h]
        col = lax.broadcasted_iota(jnp.int32, s.shape, 1)
        s = jnp.where(jnp.logical_and(first, col < DIL_TILE), NEG_INF, s)
        m = jnp.max(s, axis=-1, keepdims=True)
        p = jnp.exp(s - m)
        l = jnp.sum(p, axis=-1, keepdims=True)
        o_ref[:, hs] = _dot(p.astype(BF16), v) / l
        lse_ref[:, hs] = jnp.broadcast_to(m + jnp.log(l), (DIL_TILE, B_HEAD_DIM))


def _dil_prompt(qkv, bias, dil, batch, seq):
    sd = seq // dil
    nq = sd // DIL_TILE
    blk = (DIL_TILE, GROUP_DIM)
    cur = lambda b, i: b * nq + i
    prev = lambda b, i: b * nq + jnp.maximum(i - 1, 0)
    return pl.pallas_call(
        _dil_prompt_kernel,
        grid=(batch, dil, nq),
        in_specs=[
            pl.BlockSpec(blk, lambda b, r, i: (cur(b, i), 3 * r)),
            pl.BlockSpec(blk, lambda b, r, i: (cur(b, i), 3 * r + 1)),
            pl.BlockSpec(blk, lambda b, r, i: (prev(b, i), 3 * r + 1)),
            pl.BlockSpec(blk, lambda b, r, i: (cur(b, i), 3 * r + 2)),
            pl.BlockSpec(blk, lambda b, r, i: (prev(b, i), 3 * r + 2)),
            pl.BlockSpec((GROUP_HEADS, DIL_TILE, 2 * DIL_TILE), lambda b, r, i: (0, 0, 0)),
        ],
        out_specs=[
            pl.BlockSpec(blk, lambda b, r, i: (cur(b, i), r)),
            pl.BlockSpec(blk, lambda b, r, i: (cur(b, i), r)),
        ],
        out_shape=[
            jax.ShapeDtypeStruct((batch * sd, dil * GROUP_DIM), F32),
            jax.ShapeDtypeStruct((batch * sd, dil * GROUP_DIM), F32),
        ],
        compiler_params=_cparams("parallel", "parallel", "arbitrary"),
        name="dilated_prompt",
    )(qkv, qkv, qkv, qkv, qkv, bias)


def _dil_sample_kernel(q_ref, kn_ref, vn_ref, buf_ref, bias_ref, o_ref, lse_ref, *, dil):
    row = lax.broadcasted_iota(jnp.int32, (SUBLANES, B_HEAD_DIM), 0)
    row_wide = lax.broadcasted_iota(jnp.int32, (SUBLANES, GROUP_DIM), 0)
    own = row_wide == lax.broadcasted_iota(jnp.int32, (SUBLANES, GROUP_DIM), 1) // B_HEAD_DIM
    bias_keys = bias_ref[:, :DIL_KEYS]
    bias_new = bias_ref[:, DIL_KEYS:DIL_KEYS + 1]

    def cached(b, r):
        if dil == 1:
            return buf_ref[b, pl.ds(r, DIL_KEYS, stride=BUF_ROWS), :]
        return buf_ref[b, :, r, :]

    for b in range(DIL_SAMPLE_BLOCK):
        qb = q_ref[b:b + 1, :]
        s = jnp.zeros((SUBLANES, DIL_KEYS), F32)
        for h in range(GROUP_HEADS):
            qh = jnp.broadcast_to(qb[:, h * B_HEAD_DIM:(h + 1) * B_HEAD_DIM], (SUBLANES, B_HEAD_DIM))
            sh = _dot_nt(qh.astype(BF16), cached(b, 2 * h).astype(BF16))
            s = jnp.where(row == h, sh, s)
        s = s * B_SCALE + bias_keys
        kn = kn_ref[b:b + 1, :].astype(BF16).astype(F32)
        vn = vn_ref[b:b + 1, :].astype(BF16).astype(F32)
        q_own = jnp.where(own, qb, 0.0).astype(BF16).astype(F32)
        s_new = jnp.sum(q_own * kn, axis=-1, keepdims=True) * B_SCALE + bias_new
        m = jnp.maximum(jnp.max(s, axis=-1, keepdims=True), s_new)
        p = jnp.exp(s - m)
        p_new = jnp.exp(s_new - m)
        l = jnp.sum(p, axis=-1, keepdims=True) + p_new
        p16 = p.astype(BF16)
        pn = p_new.astype(BF16).astype(F32)
        lse = m + jnp.log(l)
        for h in range(GROUP_HEADS):
            hs = slice(h * B_HEAD_DIM, (h + 1) * B_HEAD_DIM)
            oh = (_dot(p16, cached(b, 2 * h + 1).astype(BF16)) + pn * vn[:, hs]) / l
            o_ref[b:b + 1, hs] = jnp.sum(jnp.where(row == h, oh, 0.0), axis=0, keepdims=True)
            lse_ref[b:b + 1, hs] = jnp.sum(jnp.where(row == h, lse, 0.0), axis=0, keepdims=True)


def _dil_sample(q_s, kv_s, buf, bias, g, dil):
    db, length = buf.shape[:2]
    bb = DIL_SAMPLE_BLOCK
    rows = jnp.transpose(buf, (0, 1, 3, 2, 4))
    if dil == 1:
        rows = rows.reshape(db, length * BUF_ROWS, B_HEAD_DIM)
        buf_spec = pl.BlockSpec((bb, length * BUF_ROWS, B_HEAD_DIM), lambda i: (i, 0, 0))
    else:
        rows = rows.reshape(db, length // dil, dil * BUF_ROWS, B_HEAD_DIM)
        buf_spec = pl.BlockSpec((bb, DIL_KEYS, BUF_ROWS_PAD, B_HEAD_DIM), lambda i: (i, 0, 0, 0))
    return pl.pallas_call(
        functools.partial(_dil_sample_kernel, dil=dil),
        grid=(db // bb,),
        in_specs=[
            pl.BlockSpec((bb, GROUP_DIM), lambda i: (i, g)),
            pl.BlockSpec((bb, GROUP_DIM), lambda i: (i, g)),
            pl.BlockSpec((bb, GROUP_DIM), lambda i: (i, N_GROUPS + g)),
            buf_spec,
            pl.BlockSpec((SUBLANES, 2 * DIL_KEYS), lambda i: (0, 0)),
        ],
        out_specs=[
            pl.BlockSpec((bb, GROUP_DIM), lambda i: (i, 0)),
            pl.BlockSpec((bb, GROUP_DIM), lambda i: (i, 0)),
        ],
        out_shape=[
            jax.ShapeDtypeStruct((db, GROUP_DIM), F32),
            jax.ShapeDtypeStruct((db, GROUP_DIM), F32),
        ],
        compiler_params=_cparams("parallel"),
        name="dilated_sample",
    )(q_s, kv_s, kv_s, rows, bias)


def _merge_out_kernel(o0, o1, o2, l0, l1, l2, w_ref, x_ref, g_ref, out_ref, merged_ref, *scratch, dils):
    srcs = ((o0, l0), (o1, l1), (o2, l2))
    tm = x_ref.shape[0]
    ordered = []
    si = 0
    for (o_src, l_src), dil in zip(srcs, dils):
        if dil == 1:
            ordered.append(None)
            continue
        o_dst, l_dst = scratch[si], scratch[si + 1]
        si += 2
        n = tm // dil
        for r in range(dil):
            for h in range(GROUP_HEADS):
                lo = r * GROUP_DIM + h * B_HEAD_DIM
                o_dst[h, pl.ds(r, n, stride=dil), :] = o_src[:, lo:lo + B_HEAD_DIM]
                l_dst[h, pl.ds(r, n, stride=dil), :] = l_src[:, lo:lo + B_HEAD_DIM]
        ordered.append((o_dst, l_dst))
    for h in range(GROUP_HEADS):
        hs = slice(h * B_HEAD_DIM, (h + 1) * B_HEAD_DIM)
        os_, ls_ = [], []
        for (o_src, l_src), dst in zip(srcs, ordered):
            os_.append(o_src[:, hs] if dst is None else dst[0][h])
            ls_.append(l_src[:, hs] if dst is None else dst[1][h])
        mx = jnp.maximum(jnp.maximum(ls_[0], ls_[1]), ls_[2])
        e0, e1, e2 = jnp.exp(ls_[0] - mx), jnp.exp(ls_[1] - mx), jnp.exp(ls_[2] - mx)
        merged = (e0 * os_[0] + e1 * os_[1] + e2 * os_[2]) / (e0 + e1 + e2)
        merged_ref[:, hs] = merged.astype(BF16)
    y = _dot(merged_ref[...], w_ref[...])
    out_ref[...] = x_ref[...] + _rms(y, g_ref[...])


def _merge_out(outs, lses, dils, w, x, g, tm):
    m, d = x.shape
    parts = [pl.BlockSpec((tm // dil, dil * GROUP_DIM), lambda i: (i, 0)) for dil in dils]
    scratch = [pltpu.VMEM((tm, GROUP_DIM), BF16)]
    for dil in dils:
        if dil > 1:
            scratch += [pltpu.VMEM((GROUP_HEADS, tm, B_HEAD_DIM), F32)] * 2
    return pl.pallas_call(
        functools.partial(_merge_out_kernel, dils=tuple(dils)),
        grid=(m // tm,),
        in_specs=parts + parts + [
            pl.BlockSpec((GROUP_DIM, d), lambda i: (0, 0)),
            pl.BlockSpec((tm, d), lambda i: (i, 0)),
            pl.BlockSpec((1, d), lambda i: (0, 0)),
        ],
        out_specs=pl.BlockSpec((tm, d), lambda i: (i, 0)),
        out_shape=jax.ShapeDtypeStruct((m, d), F32),
        scratch_shapes=scratch,
        compiler_params=_cparams("parallel"),
        name="dilated_merge_out",
    )(*outs, *lses, w, x, g)


def _t5_bucket(dist):
    max_exact = T5_BUCKETS // 2
    d = np.maximum(dist, 1).astype(np.float32)
    large = max_exact + (np.log(d / max_exact) / math.log(T5_MAX_DIST / max_exact)
                         * (T5_BUCKETS - max_exact)).astype(np.int32)
    large = np.minimum(large, T5_BUCKETS - 1)
    return np.where(dist < max_exact, dist, large).astype(np.int32)


def _rows_static(table, idx):
    pieces, start = [], 0
    for i in range(1, len(idx) + 1):
        if i == len(idx) or idx[i] != idx[start]:
            row = table[int(idx[start]):int(idx[start]) + 1]
            pieces.append(jnp.broadcast_to(row, (i - start,) + table.shape[1:]))
            start = i
    return jnp.concatenate(pieces, axis=0)


def _group_bias(t5_bias, g):
    window, dil = DIL_GROUPS[g]
    buckets = _t5_bucket(dil * np.arange(window // dil + 1))
    return _rows_static(t5_bias[:, g * GROUP_HEADS:(g + 1) * GROUP_HEADS].astype(F32), buckets)


def _prompt_bias(bias):
    n = 2 * DIL_TILE
    heads = bias.shape[1]
    period = jnp.concatenate([bias[::-1], jnp.full((n - DIL_KEYS, heads), NEG_INF, F32)], axis=0)
    band = jnp.tile(period, (DIL_TILE, 1))[:DIL_TILE * n].reshape(DIL_TILE, n, heads)
    return jnp.transpose(band, (2, 0, 1))


def _sample_bias(bias):
    keys = bias[1:][::-1].T
    own = jnp.broadcast_to(bias[0][:, None], (GROUP_HEADS, DIL_KEYS))
    both = jnp.concatenate([keys, own], axis=1)
    return jnp.pad(both, ((0, SUBLANES - GROUP_HEADS), (0, 0)))


def _rope_tables(pos):
    half = ROPE_DIM // 2
    inv = ROPE_THETA ** (-jnp.arange(half, dtype=F32) / half)
    ang = pos.astype(F32)[:, None] * inv
    cos, sin = jnp.cos(ang), jnp.sin(ang)
    zero = jnp.zeros((pos.shape[0], LANES - ROPE_DIM), F32)
    return (jnp.concatenate([cos, cos, zero], axis=1), jnp.concatenate([-sin, sin, zero], axis=1))


def _tile(n, pref):
    return pref if n % pref == 0 else n


def kernel(x_prompt, x_sample, cache_mla, page_table, cache_b_win0, cache_b_win1, cache_b_win2,
           norm_gains, w_mla_in, g_q_a, g_kv_a, w_uq, w_uk, w_uv, w_mla_o,
           g_kv_b, w_kv_b, w_q_b, w_b_o, t5_bias, w_ffn_in, w_ffn_out):
    batch, seq, d = x_prompt.shape
    db = x_sample.shape[0]
    assert x_sample.shape[1] == 1 and DEPTH == 2 and N_A_LAYERS == 1
    win_bufs = (cache_b_win0, cache_b_win1, cache_b_win2)
    for gi, (window, dil) in enumerate(DIL_GROUPS):
        assert win_bufs[gi].shape[1] == window and window // dil == DIL_KEYS
        assert seq % (dil * DIL_TILE) == 0
    mp = batch * seq
    xp = x_prompt.reshape(mp, d)
    xs = x_sample.reshape(db, d)
    gains = norm_gains.reshape(DEPTH, 4, 1, d)
    tm_p = _tile(mp, 1024)
    tm_h = _tile(mp, 512)

    rope_cols = jnp.pad(w_mla_in[0][:, Q_LORA + KV_LORA:], ((0, 0), (0, LANES - ROPE_DIM)))
    w_in_pad = jnp.concatenate([w_mla_in[0][:, :Q_LORA + KV_LORA], rope_cols], axis=1).astype(BF16)
    w_uq_pad = jnp.pad(w_uq[0].reshape(Q_LORA, MLA_HEADS, NOPE_DIM + ROPE_DIM),
                       ((0, 0), (0, 0), (0, HEAD_PAD - NOPE_DIM - ROPE_DIM)))
    w_uq_pad = w_uq_pad.reshape(Q_LORA, MLA_HEADS * HEAD_PAD).astype(BF16)
    w_k_lat = jnp.pad(w_uk[0], ((0, 0), (0, 0), (0, HEAD_PAD - NOPE_DIM)))
    rope_copy = jnp.pad(jnp.eye(ROPE_DIM, dtype=F32), ((0, 0), (NOPE_DIM, HEAD_PAD - NOPE_DIM - ROPE_DIM)))
    rope_copy = jnp.broadcast_to(rope_copy[:, None, :], (ROPE_DIM, MLA_HEADS, HEAD_PAD))
    w_k = jnp.concatenate([w_k_lat, rope_copy], axis=0).reshape(ROW_DIM, MLA_HEADS * HEAD_PAD)
    w_v = jnp.pad(w_uv[0].reshape(KV_LORA, MLA_HEADS * V_DIM), ((0, ROPE_DIM), (0, 0)))
    w_kv = jnp.concatenate([w_k, w_v], axis=1).astype(BF16)
    w_uk_t = jnp.transpose(w_uk[0], (1, 2, 0)).astype(BF16)
    w_uv_t = jnp.transpose(w_uv[0], (1, 0, 2)).astype(BF16)
    w_o = w_mla_o[0].astype(BF16)
    w_kvb = w_kv_b.astype(BF16)
    w_qb = w_q_b[0].astype(BF16)
    w_bo = w_b_o[0].astype(BF16)
    w_f_in = w_ffn_in.astype(BF16)
    w_f_out = w_ffn_out.astype(BF16)
    f = w_ffn_out.shape[1]
    tf = _tile(f, 512)

    c_p, s_p = _rope_tables(jnp.tile(jnp.arange(seq), batch))
    c_s, s_s = _rope_tables(jnp.full((db,), PAST_LEN, jnp.int32))

    g = gains[0]
    cq_p, rows_p = _mla_in(xp, g[0], w_in_pad, g_q_a[0:1], g_kv_a[0:1], c_p, s_p, tm_h)
    cq_s, rows_s = _mla_in(xs, g[0], w_in_pad, g_q_a[0:1], g_kv_a[0:1], c_s, s_s, db)
    q_p = _q_up(cq_p, w_uq_pad, c_p, s_p, tm_p, 1024, out_scale=MLA_SCALE * math.log2(math.e))
    q_s = _q_up(cq_s, w_uq_pad, c_s, s_s, db, 1024)

    kv_p = _cast_mm(rows_p, w_kv, BF16, tm_p, 1024)
    attn_p = _mla_prompt_attention(q_p, kv_p, batch, seq, _tile(seq, 512))
    xp = _mm_norm_res(attn_p, w_o, xp, g[1], tm_p, 512)

    q_lat = _q_absorb(q_s, w_uk_t)
    q_rope_s = q_s.reshape(db, MLA_HEADS, HEAD_PAD)[:, :, NOPE_DIM:NOPE_DIM + ROPE_DIM]
    q_cat = jnp.concatenate([jnp.transpose(q_lat, (1, 0, 2)), q_rope_s], axis=-1)
    cache_t = jnp.swapaxes(cache_mla[0], 1, 2)
    o_lat = _mla_decode(q_cat, rows_s, cache_t, page_table)
    attn_s = _v_up(jnp.transpose(o_lat, (1, 0, 2)).astype(BF16), w_uv_t)
    xs = _mm_norm_res(attn_s, w_o, xs, g[1], db, 512)

    xp = _ffn(xp, g[2], g[3], w_f_in[0], w_f_out[0], tm_h, tf)
    xs = _ffn(xs, g[2], g[3], w_f_in[0], w_f_out[0], db, tf)

    g = gains[1]
    g_kvb = g_kv_b.reshape(1, d)
    kvb_s = _norm_mm(xs, g_kvb, w_kvb, F32, db, 512)
    qb_s = _norm_mm(xs, g[0], w_qb, F32, db, GROUP_DIM)
    outs_p, lses_p, outs_s, lses_s, kv_groups = [], [], [], [], []
    dils = [dil for _, dil in DIL_GROUPS]
    biases = [_group_bias(t5_bias, gi) for gi in range(N_GROUPS)]
    band_all = _prompt_bias(jnp.concatenate(biases, axis=1))
    for gi, (window, dil) in enumerate(DIL_GROUPS):
        cols = slice(gi * GROUP_DIM, (gi + 1) * GROUP_DIM)
        w_g = jnp.concatenate([w_qb[:, cols], w_kvb[:, cols],
                               w_kvb[:, N_GROUPS * GROUP_DIM:][:, cols]], axis=1)
        qkv, kv_g = _dil_proj(xp, g[0], g_kvb, w_g, dil, tm_h)
        kv_groups.append(kv_g)
        bias = biases[gi]
        o, lse = _dil_prompt(qkv, band_all[gi * GROUP_HEADS:(gi + 1) * GROUP_HEADS], dil, batch, seq)
        outs_p.append(o)
        lses_p.append(lse)
        o, lse = _dil_sample(qb_s, kvb_s, win_bufs[gi], _sample_bias(bias), gi, dil)
        outs_s.append(o)
        lses_s.append(lse)
    xp = _merge_out(outs_p, lses_p, dils, w_bo, xp, g[1], _tile(mp, 256))
    xs = _merge_out(outs_s, lses_s, [1] * N_GROUPS, w_bo, xs, g[1], db)

    xp = _ffn(xp, g[2], g[3], w_f_in[1], w_f_out[1], tm_h, tf)
    xs = _ffn(xs, g[2], g[3], w_f_in[1], w_f_out[1], db, tf)

    kv6_s = kvb_s.reshape(db, 1, 2, N_GROUPS, GROUP_HEADS, B_HEAD_DIM)
    wins = []
    for gi, (window, _) in enumerate(DIL_GROUPS):
        kv5 = kv_groups[gi].reshape(batch, seq, 2, GROUP_HEADS, B_HEAD_DIM)
        wins.append(kv5[:, seq - min(window, seq):])
        wins.append(kv6_s[:, :, :, gi])
    return (xp.reshape(batch, seq, d), xs.reshape(db, 1, d),
            rows_p.reshape(1, batch, seq, ROW_DIM), rows_s.reshape(1, db, 1, ROW_DIM), *wins)
```

```python
import functools
import math

import numpy as np
import jax
import jax.numpy as jnp
from jax import lax
from jax.experimental import pallas as pl
from jax.experimental.pallas import tpu as pltpu

D_MODEL = 2048
DEPTH = 2
PAST_LEN = 16384
PAGE_SIZE = 128
N_A_LAYERS = DEPTH // 2
MLA_HEADS = 16
Q_LORA = 512
KV_LORA = 512
NOPE_DIM = 128
ROPE_DIM = 64
V_DIM = 128
ROPE_THETA = 10000.0
MLA_SCALE = (NOPE_DIM + ROPE_DIM) ** -0.5
DIL_GROUPS = ((128, 1), (512, 4), (2048, 16))
N_GROUPS = len(DIL_GROUPS)
GROUP_HEADS = 6
B_HEAD_DIM = 128
B_HEADS = N_GROUPS * GROUP_HEADS
B_SCALE = B_HEAD_DIM ** -0.5
T5_BUCKETS = 32
T5_MAX_DIST = 2048
EPS = 1e-6
NEG_INF = -1e30

F32 = jnp.float32
BF16 = jnp.bfloat16

VMEM_LIMIT_BYTES = 56 * 1024 * 1024
LANES = 128
SUBLANES = 8
HEAD_PAD = 2 * LANES
ROW_DIM = KV_LORA + ROPE_DIM
GROUP_DIM = GROUP_HEADS * B_HEAD_DIM
QKV_DIM = 3 * GROUP_DIM
DIL_KEYS = 128
DIL_TILE = 128
FLASH_HEADS = 4
DEC_PAGES = 32
DEC_SLOTS = 3
DIL_SAMPLE_BLOCK = 8
BUF_ROWS = 2 * GROUP_HEADS
BUF_ROWS_PAD = 16


def _cparams(*semantics):
    return pltpu.CompilerParams(dimension_semantics=semantics, vmem_limit_bytes=VMEM_LIMIT_BYTES)


def _rms(x, g):
    return x * lax.rsqrt(jnp.mean(x * x, axis=-1, keepdims=True) + EPS) * g


def _dot(a, b):
    return jnp.dot(a, b, preferred_element_type=F32)


def _dot_nt(a, b):
    return lax.dot_general(a, b, (((1,), (1,)), ((), ())), preferred_element_type=F32)


def _dot_tn(a, b):
    return lax.dot_general(a, b, (((0,), (0,)), ((), ())), preferred_element_type=F32)


def _rope128(x, c, s):
    lane = lax.broadcasted_iota(jnp.int32, x.shape, 1)
    half = ROPE_DIM // 2
    swapped = jnp.where(lane < half, pltpu.roll(x, LANES - half, 1), pltpu.roll(x, half, 1))
    return x * c + swapped * s


def _mla_in_kernel(x_ref, g_ref, w_ref, gq_ref, gkv_ref, c_ref, s_ref, cq_ref, rows_ref):
    xn = _rms(x_ref[...], g_ref[...]).astype(BF16)
    a = _dot(xn, w_ref[...])
    cq_ref[...] = _rms(a[:, :Q_LORA], gq_ref[...]).astype(BF16)
    rows_ref[:, :KV_LORA] = _rms(a[:, Q_LORA:Q_LORA + KV_LORA], gkv_ref[...])
    kr = _rope128(a[:, Q_LORA + KV_LORA:], c_ref[...], s_ref[...])
    rows_ref[:, KV_LORA:] = kr[:, :ROPE_DIM]


def _mla_in(x, g, w_pad, gq, gkv, c_tab, s_tab, tm):
    m, d = x.shape
    n = w_pad.shape[1]
    return pl.pallas_call(
        _mla_in_kernel,
        grid=(m // tm,),
        in_specs=[
            pl.BlockSpec((tm, d), lambda i: (i, 0)),
            pl.BlockSpec((1, d), lambda i: (0, 0)),
            pl.BlockSpec((d, n), lambda i: (0, 0)),
            pl.BlockSpec((1, Q_LORA), lambda i: (0, 0)),
            pl.BlockSpec((1, KV_LORA), lambda i: (0, 0)),
            pl.BlockSpec((tm, LANES), lambda i: (i, 0)),
            pl.BlockSpec((tm, LANES), lambda i: (i, 0)),
        ],
        out_specs=[
            pl.BlockSpec((tm, Q_LORA), lambda i: (i, 0)),
            pl.BlockSpec((tm, ROW_DIM), lambda i: (i, 0)),
        ],
        out_shape=[
            jax.ShapeDtypeStruct((m, Q_LORA), BF16),
            jax.ShapeDtypeStruct((m, ROW_DIM), F32),
        ],
        compiler_params=_cparams("parallel"),
        name="mla_in",
    )(x, g, w_pad, gq, gkv, c_tab, s_tab)


def _q_up_kernel(cq_ref, w_ref, c_ref, s_ref, o_ref, *, heads_per_tile, out_scale):
    a = _dot(cq_ref[...], w_ref[...])
    if out_scale != 1.0:
        a = a * out_scale
    c = c_ref[...]
    s = s_ref[...]
    for h in range(heads_per_tile):
        lo = h * HEAD_PAD
        o_ref[:, lo:lo + LANES] = a[:, lo:lo + LANES].astype(BF16)
        o_ref[:, lo + LANES:lo + HEAD_PAD] = _rope128(a[:, lo + LANES:lo + HEAD_PAD], c, s).astype(BF16)


def _q_up(cq, w_pad, c_tab, s_tab, tm, tn, out_scale=1.0):
    m, k = cq.shape
    n = w_pad.shape[1]
    return pl.pallas_call(
        functools.partial(_q_up_kernel, heads_per_tile=tn // HEAD_PAD, out_scale=out_scale),
        grid=(m // tm, n // tn),
        in_specs=[
            pl.BlockSpec((tm, k), lambda i, j: (i, 0)),
            pl.BlockSpec((k, tn), lambda i, j: (0, j)),
            pl.BlockSpec((tm, LANES), lambda i, j: (i, 0)),
            pl.BlockSpec((tm, LANES), lambda i, j: (i, 0)),
        ],
        out_specs=pl.BlockSpec((tm, tn), lambda i, j: (i, j)),
        out_shape=jax.ShapeDtypeStruct((m, n), BF16),
        compiler_params=_cparams("parallel", "parallel"),
        name="mla_q_up",
    )(cq, w_pad, c_tab, s_tab)


def _cast_mm_kernel(x_ref, w_ref, o_ref):
    o_ref[...] = _dot(x_ref[...].astype(BF16), w_ref[...]).astype(o_ref.dtype)


def _cast_mm(x, w, out_dtype, tm, tn):
    m, k = x.shape
    n = w.shape[1]
    return pl.pallas_call(
        _cast_mm_kernel,
        grid=(m // tm, n // tn),
        in_specs=[
            pl.BlockSpec((tm, k), lambda i, j: (i, 0)),
            pl.BlockSpec((k, tn), lambda i, j: (0, j)),
        ],
        out_specs=pl.BlockSpec((tm, tn), lambda i, j: (i, j)),
        out_shape=jax.ShapeDtypeStruct((m, n), out_dtype),
        compiler_params=_cparams("parallel", "parallel"),
        name="cast_mm",
    )(x, w)


def _norm_mm_kernel(x_ref, g_ref, w_ref, o_ref, xn_ref):
    @pl.when(pl.program_id(1) == 0)
    def _():
        xn_ref[...] = _rms(x_ref[...], g_ref[...]).astype(BF16)

    o_ref[...] = _dot(xn_ref[...], w_ref[...]).astype(o_ref.dtype)


def _norm_mm(x, g, w, out_dtype, tm, tn):
    m, d = x.shape
    n = w.shape[1]
    return pl.pallas_call(
        _norm_mm_kernel,
        grid=(m // tm, n // tn),
        in_specs=[
            pl.BlockSpec((tm, d), lambda i, j: (i, 0)),
            pl.BlockSpec((1, d), lambda i, j: (0, 0)),
            pl.BlockSpec((d, tn), lambda i, j: (0, j)),
        ],
        out_specs=pl.BlockSpec((tm, tn), lambda i, j: (i, j)),
        out_shape=jax.ShapeDtypeStruct((m, n), out_dtype),
        scratch_shapes=[pltpu.VMEM((tm, d), BF16)],
        compiler_params=_cparams("parallel", "arbitrary"),
        name="norm_mm",
    )(x, g, w)


def _mm_norm_res_kernel(a_ref, w_ref, x_ref, g_ref, o_ref, *, nk):
    k = pl.program_id(1)

    @pl.when(k == 0)
    def _():
        o_ref[...] = jnp.zeros(o_ref.shape, F32)

    o_ref[...] += _dot(a_ref[...], w_ref[...])

    @pl.when(k == nk - 1)
    def _():
        o_ref[...] = x_ref[...] + _rms(o_ref[...], g_ref[...])


def _mm_norm_res(a, w, x, g, tm, tk):
    m, kdim = a.shape
    d = w.shape[1]
    nk = kdim // tk
    return pl.pallas_call(
        functools.partial(_mm_norm_res_kernel, nk=nk),
        grid=(m // tm, nk),
        in_specs=[
            pl.BlockSpec((tm, tk), lambda i, k: (i, k)),
            pl.BlockSpec((tk, d), lambda i, k: (k, 0)),
            pl.BlockSpec((tm, d), lambda i, k: (i, 0)),
            pl.BlockSpec((1, d), lambda i, k: (0, 0)),
        ],
        out_specs=pl.BlockSpec((tm, d), lambda i, k: (i, 0)),
        out_shape=jax.ShapeDtypeStruct((m, d), F32),
        compiler_params=_cparams("parallel", "arbitrary"),
        name="mm_norm_res",
    )(a, w, x, g)


def _ffn_kernel(x_ref, g_in_ref, g_out_ref, wg_ref, wu_ref, wo_ref, o_ref, xn_ref, *, nf):
    j = pl.program_id(1)

    @pl.when(j == 0)
    def _():
        xn_ref[...] = _rms(x_ref[...], g_in_ref[...]).astype(BF16)
        o_ref[...] = jnp.zeros(o_ref.shape, F32)

    xn = xn_ref[...]
    tf = wo_ref.shape[0]
    half = tf // 2 if tf % (2 * LANES) == 0 else tf
    for lo in range(0, tf, half):
        gate = _dot(xn, wg_ref[:, lo:lo + half])
        up = _dot(xn, wu_ref[:, lo:lo + half])
        act = (gate * jax.nn.sigmoid(gate) * up).astype(BF16)
        o_ref[...] += _dot(act, wo_ref[lo:lo + half, :])

    @pl.when(j == nf - 1)
    def _():
        o_ref[...] = x_ref[...] + _rms(o_ref[...], g_out_ref[...])


def _ffn(x, g_in, g_out, w_in, w_out, layer, tm, tf):
    m, d = x.shape
    f = w_out.shape[1]
    nf = f // tf
    return pl.pallas_call(
        functools.partial(_ffn_kernel, nf=nf),
        grid=(m // tm, nf),
        in_specs=[
            pl.BlockSpec((tm, d), lambda i, j: (i, 0)),
            pl.BlockSpec((1, d), lambda i, j: (0, 0)),
            pl.BlockSpec((1, d), lambda i, j: (0, 0)),
            pl.BlockSpec((None, d, tf), lambda i, j: (layer, 0, j)),
            pl.BlockSpec((None, d, tf), lambda i, j: (layer, 0, j + nf)),
            pl.BlockSpec((None, tf, d), lambda i, j: (layer, j, 0)),
        ],
        out_specs=pl.BlockSpec((tm, d), lambda i, j: (i, 0)),
        out_shape=jax.ShapeDtypeStruct((m, d), F32),
        scratch_shapes=[pltpu.VMEM((tm, d), BF16)],
        compiler_params=_cparams("parallel", "arbitrary"),
        name="ffn",
    )(x, g_in, g_out, w_in, w_in, w_out)


def _flash_kernel(q_ref, k_ref, v_ref, o_ref, m_ref, l_ref, acc_ref, *, tq):
    qi = pl.program_id(2)
    heads = FLASH_HEADS
    qts = [q_ref[:, h * HEAD_PAD:(h + 1) * HEAD_PAD].astype(F32).T.astype(BF16) for h in range(heads)]
    m_ref[...] = jnp.full(m_ref.shape, -jnp.inf, F32)
    l_ref[...] = jnp.zeros(l_ref.shape, F32)
    acc_ref[...] = jnp.zeros(acc_ref.shape, F32)

    def step(kc, diagonal):
        start = pl.multiple_of(kc * tq, tq)
        scores = [_dot(k_ref[pl.ds(start, tq), h * HEAD_PAD:(h + 1) * HEAD_PAD], qts[h]) for h in range(heads)]
        for h in range(heads):
            v = v_ref[pl.ds(start, tq), h * V_DIM:(h + 1) * V_DIM]
            s = scores[h]
            if diagonal:
                key = lax.broadcasted_iota(jnp.int32, s.shape, 0)
                query = lax.broadcasted_iota(jnp.int32, s.shape, 1)
                s = jnp.where(key <= query, s, NEG_INF)
            m_prev = m_ref[h]
            m_new = jnp.maximum(m_prev, jnp.max(s, axis=0, keepdims=True))
            alpha = jnp.exp2(m_prev - m_new)
            p = jnp.exp2(s - m_new)
            l_ref[h] = alpha * l_ref[h] + jnp.sum(p, axis=0, keepdims=True)
            acc_ref[h] = alpha * acc_ref[h] + _dot_tn(v, p.astype(BF16))
            m_ref[h] = m_new

    def body(kc, carry):
        step(kc, False)
        return carry

    lax.fori_loop(0, qi, body, 0)
    step(qi, True)
    for h in range(heads):
        o_ref[:, h * V_DIM:(h + 1) * V_DIM] = (acc_ref[h] / l_ref[h]).T.astype(o_ref.dtype)


def _mla_prompt_attention(q, kv, batch, seq, tq):
    nq = seq // tq
    hb = FLASH_HEADS
    v_block0 = MLA_HEADS * HEAD_PAD // (hb * V_DIM)
    return pl.pallas_call(
        functools.partial(_flash_kernel, tq=tq),
        grid=(batch, MLA_HEADS // hb, nq),
        in_specs=[
            pl.BlockSpec((tq, hb * HEAD_PAD), lambda b, h, i: (b * nq + i, h)),
            pl.BlockSpec((seq, hb * HEAD_PAD), lambda b, h, i: (b, h)),
            pl.BlockSpec((seq, hb * V_DIM), lambda b, h, i: (b, v_block0 + h)),
        ],
        out_specs=pl.BlockSpec((tq, hb * V_DIM), lambda b, h, i: (b * nq + i, h)),
        out_shape=jax.ShapeDtypeStruct((batch * seq, MLA_HEADS * V_DIM), BF16),
        scratch_shapes=[
            pltpu.VMEM((hb, 1, tq), F32),
            pltpu.VMEM((hb, 1, tq), F32),
            pltpu.VMEM((hb, V_DIM, tq), F32),
        ],
        compiler_params=_cparams("parallel", "parallel", "arbitrary"),
        name="mla_prompt_attention",
    )(q, kv, kv)


def _head_mm_kernel(a_ref, w_ref, o_ref):
    o_ref[...] = _dot(a_ref[...], w_ref[...]).astype(o_ref.dtype)


def _q_absorb(q_s, w_uk_t):
    db = q_s.shape[0]
    return pl.pallas_call(
        _head_mm_kernel,
        grid=(MLA_HEADS,),
        in_specs=[
            pl.BlockSpec((db, NOPE_DIM), lambda h: (0, 2 * h)),
            pl.BlockSpec((None, NOPE_DIM, KV_LORA), lambda h: (h, 0, 0)),
        ],
        out_specs=pl.BlockSpec((None, db, KV_LORA), lambda h: (h, 0, 0)),
        out_shape=jax.ShapeDtypeStruct((MLA_HEADS, db, KV_LORA), BF16),
        compiler_params=_cparams("parallel"),
        name="mla_q_absorb",
    )(q_s, w_uk_t)


def _v_up(o_lat_t, w_uv_t):
    db = o_lat_t.shape[1]
    return pl.pallas_call(
        _head_mm_kernel,
        grid=(MLA_HEADS,),
        in_specs=[
            pl.BlockSpec((None, db, KV_LORA), lambda h: (h, 0, 0)),
            pl.BlockSpec((None, KV_LORA, V_DIM), lambda h: (h, 0, 0)),
        ],
        out_specs=pl.BlockSpec((db, V_DIM), lambda h: (0, h)),
        out_shape=jax.ShapeDtypeStruct((db, MLA_HEADS * V_DIM), BF16),
        compiler_params=_cparams("parallel"),
        name="mla_v_up",
    )(o_lat_t, w_uv_t)


def _mla_decode_kernel(pt_ref, q_ref, new_ref, cache_ref, o_ref, pages_ref, kvt0_ref, kvt1_ref, s0_ref, s1_ref,
                       m_ref, l_ref, acc_ref, sem, *, n_pages, n_chunks, n_steps):
    kvt_refs = (kvt0_ref, kvt1_ref)
    s_refs = (s0_ref, s1_ref)
    t = pl.program_id(0)
    prev_chunk = (t + n_chunks - 1) % n_chunks

    def page_copy(step, p):
        chunk = jnp.minimum(step, n_steps - 2)
        slot = step % DEC_SLOTS
        return pltpu.make_async_copy(cache_ref.at[pt_ref[chunk * n_pages + p]], pages_ref.at[slot, p],
                                     sem.at[slot, p])

    def start_chunk(step):
        for p in range(n_pages):
            page_copy(step, p).start(priority=p % 2)

    @pl.when(t == 0)
    def _():
        for step in range(DEC_SLOTS - 1):
            start_chunk(step)
        kvt_refs[1][...] = jnp.zeros(kvt_refs[1].shape, BF16)
        s_refs[1][...] = jnp.zeros(s_refs[1].shape, F32)
        m_ref[...] = jnp.zeros(m_ref.shape, F32)
        l_ref[...] = jnp.zeros(l_ref.shape, F32)
        acc_ref[...] = jnp.zeros(acc_ref.shape, F32)

    @pl.when(jnp.logical_and(t > 0, prev_chunk == 0))
    def _():
        new = new_ref[...].astype(BF16).astype(F32)
        m_ref[...] = jnp.sum(q_ref[...].astype(F32) * new, axis=-1, keepdims=True) * MLA_SCALE
        l_ref[...] = jnp.ones(l_ref.shape, F32)
        acc_ref[...] = jnp.broadcast_to(new, acc_ref.shape)

    def work(cur, prev):
        kvt_cur, kvt_prev = kvt_refs[cur], kvt_refs[prev]
        slot = t % DEC_SLOTS
        start_chunk(t + DEC_SLOTS - 1)
        for p in range(n_pages):
            page_copy(t, p).wait()
            kvt_cur[:, p * PAGE_SIZE:(p + 1) * PAGE_SIZE] = pages_ref[slot, p].astype(BF16)
        s_refs[cur][...] = _dot(q_ref[...], kvt_cur[...]) * MLA_SCALE
        s = s_refs[prev][...]
        m_prev = m_ref[...]
        m_new = jnp.maximum(m_prev, jnp.max(s, axis=-1, keepdims=True))
        alpha = jnp.exp(m_prev - m_new)
        p = jnp.exp(s - m_new)
        l_ref[...] = alpha * l_ref[...] + jnp.sum(p, axis=-1, keepdims=True)
        acc_ref[...] = alpha * acc_ref[...] + _dot_nt(p.astype(BF16), kvt_prev[...])
        m_ref[...] = m_new

    @pl.when(t % 2 == 0)
    def _():
        work(0, 1)

    @pl.when(t % 2 == 1)
    def _():
        work(1, 0)

    @pl.when(jnp.logical_and(t > 0, prev_chunk == n_chunks - 1))
    def _():
        o_ref[...] = (acc_ref[...] / l_ref[...])[:, :KV_LORA]

    @pl.when(t == n_steps - 1)
    def _():
        for step in range(n_steps, n_steps + DEC_SLOTS - 1):
            for p in range(n_pages):
                page_copy(step, p).wait()


def _mla_decode(q_cat, rows_new, cache_t, page_table):
    db = q_cat.shape[0]
    n_tab = page_table.shape[1]
    n_pages = min(DEC_PAGES, n_tab // 2)
    n_chunks = n_tab // n_pages
    assert n_chunks >= 2 and n_chunks * n_pages == n_tab
    n_steps = db * n_chunks + 1
    width = n_pages * PAGE_SIZE
    chunk = lambda t: jnp.minimum(t, n_steps - 2)
    batch_of = lambda t: chunk(t) // n_chunks

    grid_spec = pltpu.PrefetchScalarGridSpec(
        num_scalar_prefetch=1,
        grid=(n_steps,),
        in_specs=[
            pl.BlockSpec((None, MLA_HEADS, ROW_DIM), lambda t, pt: (batch_of(t), 0, 0)),
            pl.BlockSpec((None, 1, ROW_DIM), lambda t, pt: (batch_of(t), 0, 0)),
            pl.BlockSpec(memory_space=pl.ANY),
        ],
        out_specs=pl.BlockSpec((None, MLA_HEADS, KV_LORA),
                               lambda t, pt: (jnp.maximum(t - 1, 0) // n_chunks, 0, 0)),
        scratch_shapes=[
            pltpu.VMEM((DEC_SLOTS, n_pages, ROW_DIM, PAGE_SIZE), F32),
            pltpu.VMEM((ROW_DIM, width), BF16),
            pltpu.VMEM((ROW_DIM, width), BF16),
            pltpu.VMEM((MLA_HEADS, width), F32),
            pltpu.VMEM((MLA_HEADS, width), F32),
            pltpu.VMEM((MLA_HEADS, 1), F32),
            pltpu.VMEM((MLA_HEADS, 1), F32),
            pltpu.VMEM((MLA_HEADS, ROW_DIM), F32),
            pltpu.SemaphoreType.DMA((DEC_SLOTS, n_pages)),
        ],
    )
    return pl.pallas_call(
        functools.partial(_mla_decode_kernel, n_pages=n_pages, n_chunks=n_chunks, n_steps=n_steps),
        grid_spec=grid_spec,
        out_shape=jax.ShapeDtypeStruct((db, MLA_HEADS, KV_LORA), F32),
        compiler_params=_cparams("arbitrary"),
        name="mla_decode",
    )(page_table.reshape(-1), q_cat, rows_new.reshape(db, 1, ROW_DIM), cache_t)


def _dil_proj_kernel(x_ref, gq_ref, gkv_ref, wq_ref, wk_ref, wv_ref, qkv_ref, kv_ref, y_ref, *, dil):
    x = x_ref[...]
    xr = x * lax.rsqrt(jnp.mean(x * x, axis=-1, keepdims=True) + EPS)
    xq = (xr * gq_ref[...]).astype(BF16)
    xkv = (xr * gkv_ref[...]).astype(BF16)
    tm = x.shape[0]
    n = tm // dil
    tw = kv_ref.shape[0]
    for j, w_ref in enumerate((wq_ref, wk_ref, wv_ref)):
        y = _dot(xq if j == 0 else xkv, w_ref[...])
        if j > 0:
            kv_ref[:, (j - 1) * GROUP_DIM:j * GROUP_DIM] = y[tm - tw:, :]
        if dil == 1:
            qkv_ref[:, j * GROUP_DIM:(j + 1) * GROUP_DIM] = y.astype(BF16)
            continue
        for h in range(GROUP_HEADS):
            y_ref[h] = y[:, h * B_HEAD_DIM:(h + 1) * B_HEAD_DIM]
        for r in range(dil):
            for h in range(GROUP_HEADS):
                lo = r * QKV_DIM + j * GROUP_DIM + h * B_HEAD_DIM
                qkv_ref[:, lo:lo + B_HEAD_DIM] = y_ref[h, pl.ds(r, n, stride=dil), :].astype(BF16)


def _dil_proj(x, gq, gkv, w_q, w_kv, g, dil, window, seq, tm):
    m, d = x.shape
    tiles_per_seq = seq // tm
    tw = min(tm, window)
    win_tiles = window // tw
    assert window <= seq and window % tw == 0 and tm % tw == 0

    def win_block(i):
        return (i // tiles_per_seq) * win_tiles + jnp.maximum(i % tiles_per_seq - (tiles_per_seq - win_tiles), 0)

    w_block = (d, GROUP_DIM)
    return pl.pallas_call(
        functools.partial(_dil_proj_kernel, dil=dil),
        grid=(m // tm,),
        in_specs=[
            pl.BlockSpec((tm, d), lambda i: (i, 0)),
            pl.BlockSpec((1, d), lambda i: (0, 0)),
            pl.BlockSpec((1, d), lambda i: (0, 0)),
            pl.BlockSpec(w_block, lambda i: (0, g)),
            pl.BlockSpec(w_block, lambda i: (0, g)),
            pl.BlockSpec(w_block, lambda i: (0, N_GROUPS + g)),
        ],
        out_specs=[
            pl.BlockSpec((tm // dil, dil * QKV_DIM), lambda i: (i, 0)),
            pl.BlockSpec((tw, 2 * GROUP_DIM), lambda i: (win_block(i), 0)),
        ],
        out_shape=[
            jax.ShapeDtypeStruct((m // dil, dil * QKV_DIM), BF16),
            jax.ShapeDtypeStruct((m // seq * window, 2 * GROUP_DIM), F32),
        ],
        scratch_shapes=[pltpu.VMEM((GROUP_HEADS, tm, B_HEAD_DIM), F32)],
        compiler_params=_cparams("arbitrary"),
        name="dilated_proj",
    )(x, gq, gkv, w_q, w_kv, w_kv)


def _dil_prompt_kernel(q_ref, kc_ref, kp_ref, vc_ref, vp_ref, bias_ref, o_ref, lse_ref):
    first = pl.program_id(2) == 0
    for h in range(GROUP_HEADS):
        hs = slice(h * B_HEAD_DIM, (h + 1) * B_HEAD_DIM)
        q = q_ref[:, hs]
        k = jnp.concatenate([kp_ref[:, hs], kc_ref[:, hs]], axis=0)
        v = jnp.concatenate([vp_ref[:, hs], vc_ref[:, hs]], axis=0)
        s = _dot_nt(q, k) * B_SCALE + bias_ref[h]
        col = lax.broadcasted_iota(jnp.int32, s.shape, 1)
        s = jnp.where(jnp.logical_and(first, col < DIL_TILE), NEG_INF, s)
        m = jnp.max(s, axis=-1, keepdims=True)
        p = jnp.exp(s - m)
        l = jnp.sum(p, axis=-1, keepdims=True)
        o_ref[:, hs] = _dot(p.astype(BF16), v) / l
        lse_ref[:, hs] = jnp.broadcast_to(m + jnp.log(l), (DIL_TILE, B_HEAD_DIM))


def _dil_prompt(qkv, bias, dil, batch, seq):
    sd = seq // dil
    nq = sd // DIL_TILE
    blk = (DIL_TILE, GROUP_DIM)
    cur = lambda b, i: b * nq + i
    prev = lambda b, i: b * nq + jnp.maximum(i - 1, 0)
    return pl.pallas_call(
        _dil_prompt_kernel,
        grid=(batch, dil, nq),
        in_specs=[
            pl.BlockSpec(blk, lambda b, r, i: (cur(b, i), 3 * r)),
            pl.BlockSpec(blk, lambda b, r, i: (cur(b, i), 3 * r + 1)),
            pl.BlockSpec(blk, lambda b, r, i: (prev(b, i), 3 * r + 1)),
            pl.BlockSpec(blk, lambda b, r, i: (cur(b, i), 3 * r + 2)),
            pl.BlockSpec(blk, lambda b, r, i: (prev(b, i), 3 * r + 2)),
            pl.BlockSpec((GROUP_HEADS, DIL_TILE, 2 * DIL_TILE), lambda b, r, i: (0, 0, 0)),
        ],
        out_specs=[
            pl.BlockSpec(blk, lambda b, r, i: (cur(b, i), r)),
            pl.BlockSpec(blk, lambda b, r, i: (cur(b, i), r)),
        ],
        out_shape=[
            jax.ShapeDtypeStruct((batch * sd, dil * GROUP_DIM), F32),
            jax.ShapeDtypeStruct((batch * sd, dil * GROUP_DIM), F32),
        ],
        compiler_params=_cparams("parallel", "parallel", "arbitrary"),
        name="dilated_prompt",
    )(qkv, qkv, qkv, qkv, qkv, bias)


def _dil_sample_kernel(q_ref, kn_ref, vn_ref, buf_ref, bias_ref, o_ref, lse_ref, *, dil):
    row = lax.broadcasted_iota(jnp.int32, (SUBLANES, B_HEAD_DIM), 0)
    row_wide = lax.broadcasted_iota(jnp.int32, (SUBLANES, GROUP_DIM), 0)
    own = row_wide == lax.broadcasted_iota(jnp.int32, (SUBLANES, GROUP_DIM), 1) // B_HEAD_DIM
    bias_keys = bias_ref[:, :DIL_KEYS]
    bias_new = bias_ref[:, DIL_KEYS:DIL_KEYS + 1]

    def cached(b, r):
        if dil == 1:
            return buf_ref[b, pl.ds(r, DIL_KEYS, stride=BUF_ROWS), :]
        return buf_ref[b, :, r, :]

    for b in range(DIL_SAMPLE_BLOCK):
        qb = q_ref[b:b + 1, :]
        s = jnp.zeros((SUBLANES, DIL_KEYS), F32)
        for h in range(GROUP_HEADS):
            qh = jnp.broadcast_to(qb[:, h * B_HEAD_DIM:(h + 1) * B_HEAD_DIM], (SUBLANES, B_HEAD_DIM))
            sh = _dot_nt(qh.astype(BF16), cached(b, 2 * h).astype(BF16))
            s = jnp.where(row == h, sh, s)
        s = s * B_SCALE + bias_keys
        kn = kn_ref[b:b + 1, :].astype(BF16).astype(F32)
        vn = vn_ref[b:b + 1, :].astype(BF16).astype(F32)
        q_own = jnp.where(own, qb, 0.0).astype(BF16).astype(F32)
        s_new = jnp.sum(q_own * kn, axis=-1, keepdims=True) * B_SCALE + bias_new
        m = jnp.maximum(jnp.max(s, axis=-1, keepdims=True), s_new)
        p = jnp.exp(s - m)
        p_new = jnp.exp(s_new - m)
        l = jnp.sum(p, axis=-1, keepdims=True) + p_new
        p16 = p.astype(BF16)
        pn = p_new.astype(BF16).astype(F32)
        lse = m + jnp.log(l)
        for h in range(GROUP_HEADS):
            hs = slice(h * B_HEAD_DIM, (h + 1) * B_HEAD_DIM)
            oh = (_dot(p16, cached(b, 2 * h + 1).astype(BF16)) + pn * vn[:, hs]) / l
            o_ref[b:b + 1, hs] = jnp.sum(jnp.where(row == h, oh, 0.0), axis=0, keepdims=True)
            lse_ref[b:b + 1, hs] = jnp.sum(jnp.where(row == h, lse, 0.0), axis=0, keepdims=True)


def _dil_sample(q_s, kv_s, buf, bias, g, dil):
    db, length = buf.shape[:2]
    bb = DIL_SAMPLE_BLOCK
    rows = jnp.transpose(buf, (0, 1, 3, 2, 4))
    if dil == 1:
        rows = rows.reshape(db, length * BUF_ROWS, B_HEAD_DIM)
        buf_spec = pl.BlockSpec((bb, length * BUF_ROWS, B_HEAD_DIM), lambda i: (i, 0, 0))
    else:
        rows = rows.reshape(db, length // dil, dil * BUF_ROWS, B_HEAD_DIM)
        buf_spec = pl.BlockSpec((bb, DIL_KEYS, BUF_ROWS_PAD, B_HEAD_DIM), lambda i: (i, 0, 0, 0))
    return pl.pallas_call(
        functools.partial(_dil_sample_kernel, dil=dil),
        grid=(db // bb,),
        in_specs=[
            pl.BlockSpec((bb, GROUP_DIM), lambda i: (i, g)),
            pl.BlockSpec((bb, GROUP_DIM), lambda i: (i, g)),
            pl.BlockSpec((bb, GROUP_DIM), lambda i: (i, N_GROUPS + g)),
            buf_spec,
            pl.BlockSpec((SUBLANES, 2 * DIL_KEYS), lambda i: (0, 0)),
        ],
        out_specs=[
            pl.BlockSpec((bb, GROUP_DIM), lambda i: (i, 0)),
            pl.BlockSpec((bb, GROUP_DIM), lambda i: (i, 0)),
        ],
        out_shape=[
            jax.ShapeDtypeStruct((db, GROUP_DIM), F32),
            jax.ShapeDtypeStruct((db, GROUP_DIM), F32),
        ],
        compiler_params=_cparams("parallel"),
        name="dilated_sample",
    )(q_s, kv_s, kv_s, rows, bias)


def _merge_out_kernel(o0, o1, o2, l0, l1, l2, w_ref, x_ref, g_ref, out_ref, merged_ref, *scratch, dils):
    srcs = ((o0, l0), (o1, l1), (o2, l2))
    tm = x_ref.shape[0]
    ordered = []
    si = 0
    for (o_src, l_src), dil in zip(srcs, dils):
        if dil == 1:
            ordered.append(None)
            continue
        o_dst, l_dst = scratch[si], scratch[si + 1]
        si += 2
        n = tm // dil
        for r in range(dil):
            for h in range(GROUP_HEADS):
                lo = r * GROUP_DIM + h * B_HEAD_DIM
                o_dst[h, pl.ds(r, n, stride=dil), :] = o_src[:, lo:lo + B_HEAD_DIM]
                l_dst[h, pl.ds(r, n, stride=dil), :] = l_src[:, lo:lo + B_HEAD_DIM]
        ordered.append((o_dst, l_dst))
    for h in range(GROUP_HEADS):
        hs = slice(h * B_HEAD_DIM, (h + 1) * B_HEAD_DIM)
        os_, ls_ = [], []
        for (o_src, l_src), dst in zip(srcs, ordered):
            os_.append(o_src[:, hs] if dst is None else dst[0][h])
            ls_.append(l_src[:, hs] if dst is None else dst[1][h])
        mx = jnp.maximum(jnp.maximum(ls_[0], ls_[1]), ls_[2])
        e0, e1, e2 = jnp.exp(ls_[0] - mx), jnp.exp(ls_[1] - mx), jnp.exp(ls_[2] - mx)
        merged = (e0 * os_[0] + e1 * os_[1] + e2 * os_[2]) / (e0 + e1 + e2)
        merged_ref[:, hs] = merged.astype(BF16)
    y = _dot(merged_ref[...], w_ref[...])
    out_ref[...] = x_ref[...] + _rms(y, g_ref[...])


def _merge_out(outs, lses, dils, w, x, g, tm):
    m, d = x.shape
    parts = [pl.BlockSpec((tm // dil, dil * GROUP_DIM), lambda i: (i, 0)) for dil in dils]
    scratch = [pltpu.VMEM((tm, GROUP_DIM), BF16)]
    for dil in dils:
        if dil > 1:
            scratch += [pltpu.VMEM((GROUP_HEADS, tm, B_HEAD_DIM), F32)] * 2
    return pl.pallas_call(
        functools.partial(_merge_out_kernel, dils=tuple(dils)),
        grid=(m // tm,),
        in_specs=parts + parts + [
            pl.BlockSpec((GROUP_DIM, d), lambda i: (0, 0)),
            pl.BlockSpec((tm, d), lambda i: (i, 0)),
            pl.BlockSpec((1, d), lambda i: (0, 0)),
        ],
        out_specs=pl.BlockSpec((tm, d), lambda i: (i, 0)),
        out_shape=jax.ShapeDtypeStruct((m, d), F32),
        scratch_shapes=scratch,
        compiler_params=_cparams("parallel"),
        name="dilated_merge_out",
    )(*outs, *lses, w, x, g)


def _t5_bucket(dist):
    max_exact = T5_BUCKETS // 2
    d = np.maximum(dist, 1).astype(np.float32)
    large = max_exact + (np.log(d / max_exact) / math.log(T5_MAX_DIST / max_exact)
                         * (T5_BUCKETS - max_exact)).astype(np.int32)
    large = np.minimum(large, T5_BUCKETS - 1)
    return np.where(dist < max_exact, dist, large).astype(np.int32)


def _rows_static(table, idx):
    pieces, start = [], 0
    for i in range(1, len(idx) + 1):
        if i == len(idx) or idx[i] != idx[start]:
            row = table[int(idx[start]):int(idx[start]) + 1]
            pieces.append(jnp.broadcast_to(row, (i - start,) + table.shape[1:]))
            start = i
    return jnp.concatenate(pieces, axis=0)


def _group_bias(t5_bias, g):
    window, dil = DIL_GROUPS[g]
    buckets = _t5_bucket(dil * np.arange(window // dil + 1))
    return _rows_static(t5_bias[:, g * GROUP_HEADS:(g + 1) * GROUP_HEADS].astype(F32), buckets)


def _prompt_bias(bias):
    n = 2 * DIL_TILE
    heads = bias.shape[1]
    period = jnp.concatenate([bias[::-1], jnp.full((n - DIL_KEYS, heads), NEG_INF, F32)], axis=0)
    band = jnp.tile(period, (DIL_TILE, 1))[:DIL_TILE * n].reshape(DIL_TILE, n, heads)
    return jnp.transpose(band, (2, 0, 1))


def _sample_bias(bias):
    keys = bias[1:][::-1].T
    own = jnp.broadcast_to(bias[0][:, None], (GROUP_HEADS, DIL_KEYS))
    both = jnp.concatenate([keys, own], axis=1)
    return jnp.pad(both, ((0, SUBLANES - GROUP_HEADS), (0, 0)))


def _rope_tables(pos):
    half = ROPE_DIM // 2
    inv = ROPE_THETA ** (-jnp.arange(half, dtype=F32) / half)
    ang = pos.astype(F32)[:, None] * inv
    cos, sin = jnp.cos(ang), jnp.sin(ang)
    zero = jnp.zeros((pos.shape[0], LANES - ROPE_DIM), F32)
    return (jnp.concatenate([cos, cos, zero], axis=1), jnp.concatenate([-sin, sin, zero], axis=1))


def _tile(n, pref):
    return pref if n % pref == 0 else n


def kernel(x_prompt, x_sample, cache_mla, page_table, cache_b_win0, cache_b_win1, cache_b_win2,
           norm_gains, w_mla_in, g_q_a, g_kv_a, w_uq, w_uk, w_uv, w_mla_o,
           g_kv_b, w_kv_b, w_q_b, w_b_o, t5_bias, w_ffn_in, w_ffn_out):
    batch, seq, d = x_prompt.shape
    db = x_sample.shape[0]
    assert x_sample.shape[1] == 1 and DEPTH == 2 and N_A_LAYERS == 1
    win_bufs = (cache_b_win0, cache_b_win1, cache_b_win2)
    for gi, (window, dil) in enumerate(DIL_GROUPS):
        assert win_bufs[gi].shape[1] == window and window // dil == DIL_KEYS
        assert seq % (dil * DIL_TILE) == 0
    mp = batch * seq
    xp = x_prompt.reshape(mp, d)
    xs = x_sample.reshape(db, d)
    gains = norm_gains.reshape(DEPTH, 4, 1, d)
    tm_p = _tile(mp, 1024)
    tm_h = _tile(mp, 512)

    rope_cols = jnp.pad(w_mla_in[0][:, Q_LORA + KV_LORA:], ((0, 0), (0, LANES - ROPE_DIM)))
    w_in_pad = jnp.concatenate([w_mla_in[0][:, :Q_LORA + KV_LORA], rope_cols], axis=1).astype(BF16)
    w_uq_pad = jnp.pad(w_uq[0].reshape(Q_LORA, MLA_HEADS, NOPE_DIM + ROPE_DIM),
                       ((0, 0), (0, 0), (0, HEAD_PAD - NOPE_DIM - ROPE_DIM)))
    w_uq_pad = w_uq_pad.reshape(Q_LORA, MLA_HEADS * HEAD_PAD).astype(BF16)
    w_k_lat = jnp.pad(w_uk[0], ((0, 0), (0, 0), (0, HEAD_PAD - NOPE_DIM)))
    rope_copy = jnp.pad(jnp.eye(ROPE_DIM, dtype=F32), ((0, 0), (NOPE_DIM, HEAD_PAD - NOPE_DIM - ROPE_DIM)))
    rope_copy = jnp.broadcast_to(rope_copy[:, None, :], (ROPE_DIM, MLA_HEADS, HEAD_PAD))
    w_k = jnp.concatenate([w_k_lat, rope_copy], axis=0).reshape(ROW_DIM, MLA_HEADS * HEAD_PAD)
    w_v = jnp.pad(w_uv[0].reshape(KV_LORA, MLA_HEADS * V_DIM), ((0, ROPE_DIM), (0, 0)))
    w_kv = jnp.concatenate([w_k, w_v], axis=1).astype(BF16)
    w_uk_t = jnp.transpose(w_uk[0], (1, 2, 0)).astype(BF16)
    w_uv_t = jnp.transpose(w_uv[0], (1, 0, 2)).astype(BF16)
    w_o = w_mla_o[0].astype(BF16)
    w_kvb = w_kv_b.astype(BF16)
    w_qb = w_q_b[0].astype(BF16)
    w_bo = w_b_o[0].astype(BF16)
    w_f_in = w_ffn_in.astype(BF16)
    w_f_out = w_ffn_out.astype(BF16)
    f = w_ffn_out.shape[1]
    tf = _tile(f, 512)

    c_p, s_p = _rope_tables(jnp.tile(jnp.arange(seq), batch))
    c_s, s_s = _rope_tables(jnp.full((db,), PAST_LEN, jnp.int32))

    g = gains[0]
    cq_p, rows_p = _mla_in(xp, g[0], w_in_pad, g_q_a[0:1], g_kv_a[0:1], c_p, s_p, tm_h)
    cq_s, rows_s = _mla_in(xs, g[0], w_in_pad, g_q_a[0:1], g_kv_a[0:1], c_s, s_s, db)
    q_p = _q_up(cq_p, w_uq_pad, c_p, s_p, tm_p, 1024, out_scale=MLA_SCALE * math.log2(math.e))
    q_s = _q_up(cq_s, w_uq_pad, c_s, s_s, db, 1024)

    kv_p = _cast_mm(rows_p, w_kv, BF16, tm_p, 1024)
    attn_p = _mla_prompt_attention(q_p, kv_p, batch, seq, _tile(seq, 512))
    xp = _mm_norm_res(attn_p, w_o, xp, g[1], tm_p, 512)

    q_lat = _q_absorb(q_s, w_uk_t)
    q_rope_s = q_s.reshape(db, MLA_HEADS, HEAD_PAD)[:, :, NOPE_DIM:NOPE_DIM + ROPE_DIM]
    q_cat = jnp.concatenate([jnp.transpose(q_lat, (1, 0, 2)), q_rope_s], axis=-1)
    cache_t = jnp.swapaxes(cache_mla[0], 1, 2)
    o_lat = _mla_decode(q_cat, rows_s, cache_t, page_table)
    attn_s = _v_up(jnp.transpose(o_lat, (1, 0, 2)).astype(BF16), w_uv_t)
    xs = _mm_norm_res(attn_s, w_o, xs, g[1], db, 512)

    xp = _ffn(xp, g[2], g[3], w_f_in, w_f_out, 0, tm_h, tf)
    xs = _ffn(xs, g[2], g[3], w_f_in, w_f_out, 0, db, tf)

    g = gains[1]
    g_kvb = g_kv_b.reshape(1, d)
    kvb_s = _norm_mm(xs, g_kvb, w_kvb, F32, db, 512)
    qb_s = _norm_mm(xs, g[0], w_qb, F32, db, GROUP_DIM)
    outs_p, lses_p, outs_s, lses_s, kv_groups = [], [], [], [], []
    dils = [dil for _, dil in DIL_GROUPS]
    biases = [_group_bias(t5_bias, gi) for gi in range(N_GROUPS)]
    band_all = _prompt_bias(jnp.concatenate(biases, axis=1))
    for gi, (window, dil) in enumerate(DIL_GROUPS):
        qkv, kv_g = _dil_proj(xp, g[0], g_kvb, w_qb, w_kvb, gi, dil, window, seq, tm_h)
        kv_groups.append(kv_g)
        bias = biases[gi]
        o, lse = _dil_prompt(qkv, band_all[gi * GROUP_HEADS:(gi + 1) * GROUP_HEADS], dil, batch, seq)
        outs_p.append(o)
        lses_p.append(lse)
        o, lse = _dil_sample(qb_s, kvb_s, win_bufs[gi], _sample_bias(bias), gi, dil)
        outs_s.append(o)
        lses_s.append(lse)
    xp = _merge_out(outs_p, lses_p, dils, w_bo, xp, g[1], _tile(mp, 256))
    xs = _merge_out(outs_s, lses_s, [1] * N_GROUPS, w_bo, xs, g[1], db)

    xp = _ffn(xp, g[2], g[3], w_f_in, w_f_out, 1, tm_h, tf)
    xs = _ffn(xs, g[2], g[3], w_f_in, w_f_out, 1, db, tf)

    kv6_s = kvb_s.reshape(db, 1, 2, N_GROUPS, GROUP_HEADS, B_HEAD_DIM)
    wins = []
    for gi, (window, _) in enumerate(DIL_GROUPS):
        wins.append(kv_groups[gi].reshape(batch, window, 2, GROUP_HEADS, B_HEAD_DIM))
        wins.append(kv6_s[:, :, :, gi])
    return (xp.reshape(batch, seq, d), xs.reshape(db, 1, d),
            rows_p.reshape(1, batch, seq, ROW_DIM), rows_s.reshape(1, db, 1, ROW_DIM), *wins)
```

```python
import functools
import math

import numpy as np
import jax
import jax.numpy as jnp
from jax import lax
from jax.experimental import pallas as pl
from jax.experimental.pallas import tpu as pltpu

D_MODEL = 2048
DEPTH = 2
PAST_LEN = 16384
PAGE_SIZE = 128
N_A_LAYERS = DEPTH // 2
MLA_HEADS = 16
Q_LORA = 512
KV_LORA = 512
NOPE_DIM = 128
ROPE_DIM = 64
V_DIM = 128
ROPE_THETA = 10000.0
MLA_SCALE = (NOPE_DIM + ROPE_DIM) ** -0.5
DIL_GROUPS = ((128, 1), (512, 4), (2048, 16))
N_GROUPS = len(DIL_GROUPS)
GROUP_HEADS = 6
B_HEAD_DIM = 128
B_HEADS = N_GROUPS * GROUP_HEADS
B_SCALE = B_HEAD_DIM ** -0.5
T5_BUCKETS = 32
T5_MAX_DIST = 2048
EPS = 1e-6
NEG_INF = -1e30

F32 = jnp.float32
BF16 = jnp.bfloat16

VMEM_LIMIT_BYTES = 56 * 1024 * 1024
LANES = 128
SUBLANES = 8
HEAD_PAD = 2 * LANES
ROW_DIM = KV_LORA + ROPE_DIM
GROUP_DIM = GROUP_HEADS * B_HEAD_DIM
QKV_DIM = 3 * GROUP_DIM
DIL_KEYS = 128
DIL_TILE = 128
FLASH_HEADS = 4
DEC_PAGES = 32
DEC_SLOTS = 3
DIL_SAMPLE_BLOCK = 8
BUF_ROWS = 2 * GROUP_HEADS
BUF_ROWS_PAD = 16


def _cparams(*semantics):
    return pltpu.CompilerParams(dimension_semantics=semantics, vmem_limit_bytes=VMEM_LIMIT_BYTES)


def _rms(x, g):
    return x * lax.rsqrt(jnp.mean(x * x, axis=-1, keepdims=True) + EPS) * g


def _dot(a, b):
    return jnp.dot(a, b, preferred_element_type=F32)


def _dot_nt(a, b):
    return lax.dot_general(a, b, (((1,), (1,)), ((), ())), preferred_element_type=F32)


def _dot_tn(a, b):
    return lax.dot_general(a, b, (((0,), (0,)), ((), ())), preferred_element_type=F32)


def _rope128(x, c, s):
    lane = lax.broadcasted_iota(jnp.int32, x.shape, 1)
    half = ROPE_DIM // 2
    swapped = jnp.where(lane < half, pltpu.roll(x, LANES - half, 1), pltpu.roll(x, half, 1))
    return x * c + swapped * s


def _mla_in_kernel(x_ref, g_ref, w_ref, gq_ref, gkv_ref, c_ref, s_ref, cq_ref, rows_ref):
    xn = _rms(x_ref[...], g_ref[...]).astype(BF16)
    a = _dot(xn, w_ref[...])
    cq_ref[...] = _rms(a[:, :Q_LORA], gq_ref[...]).astype(BF16)
    rows_ref[:, :KV_LORA] = _rms(a[:, Q_LORA:Q_LORA + KV_LORA], gkv_ref[...])
    kr = _rope128(a[:, Q_LORA + KV_LORA:], c_ref[...], s_ref[...])
    rows_ref[:, KV_LORA:] = kr[:, :ROPE_DIM]


def _mla_in(x, g, w_pad, gq, gkv, c_tab, s_tab, tm):
    m, d = x.shape
    n = w_pad.shape[1]
    return pl.pallas_call(
        _mla_in_kernel,
        grid=(m // tm,),
        in_specs=[
            pl.BlockSpec((tm, d), lambda i: (i, 0)),
            pl.BlockSpec((1, d), lambda i: (0, 0)),
            pl.BlockSpec((d, n), lambda i: (0, 0)),
            pl.BlockSpec((1, Q_LORA), lambda i: (0, 0)),
            pl.BlockSpec((1, KV_LORA), lambda i: (0, 0)),
            pl.BlockSpec((tm, LANES), lambda i: (i, 0)),
            pl.BlockSpec((tm, LANES), lambda i: (i, 0)),
        ],
        out_specs=[
            pl.BlockSpec((tm, Q_LORA), lambda i: (i, 0)),
            pl.BlockSpec((tm, ROW_DIM), lambda i: (i, 0)),
        ],
        out_shape=[
            jax.ShapeDtypeStruct((m, Q_LORA), BF16),
            jax.ShapeDtypeStruct((m, ROW_DIM), F32),
        ],
        compiler_params=_cparams("parallel"),
        name="mla_in",
    )(x, g, w_pad, gq, gkv, c_tab, s_tab)


def _q_up_kernel(cq_ref, w_ref, c_ref, s_ref, o_ref, *, heads_per_tile, out_scale):
    a = _dot(cq_ref[...], w_ref[...])
    if out_scale != 1.0:
        a = a * out_scale
    c = c_ref[...]
    s = s_ref[...]
    for h in range(heads_per_tile):
        lo = h * HEAD_PAD
        o_ref[:, lo:lo + LANES] = a[:, lo:lo + LANES].astype(BF16)
        o_ref[:, lo + LANES:lo + HEAD_PAD] = _rope128(a[:, lo + LANES:lo + HEAD_PAD], c, s).astype(BF16)


def _q_up(cq, w_pad, c_tab, s_tab, tm, tn, out_scale=1.0):
    m, k = cq.shape
    n = w_pad.shape[1]
    return pl.pallas_call(
        functools.partial(_q_up_kernel, heads_per_tile=tn // HEAD_PAD, out_scale=out_scale),
        grid=(m // tm, n // tn),
        in_specs=[
            pl.BlockSpec((tm, k), lambda i, j: (i, 0)),
            pl.BlockSpec((k, tn), lambda i, j: (0, j)),
            pl.BlockSpec((tm, LANES), lambda i, j: (i, 0)),
            pl.BlockSpec((tm, LANES), lambda i, j: (i, 0)),
        ],
        out_specs=pl.BlockSpec((tm, tn), lambda i, j: (i, j)),
        out_shape=jax.ShapeDtypeStruct((m, n), BF16),
        compiler_params=_cparams("parallel", "parallel"),
        name="mla_q_up",
    )(cq, w_pad, c_tab, s_tab)


def _cast_mm_kernel(x_ref, w_ref, o_ref):
    o_ref[...] = _dot(x_ref[...].astype(BF16), w_ref[...]).astype(o_ref.dtype)


def _cast_mm(x, w, out_dtype, tm, tn):
    m, k = x.shape
    n = w.shape[1]
    return pl.pallas_call(
        _cast_mm_kernel,
        grid=(m // tm, n // tn),
        in_specs=[
            pl.BlockSpec((tm, k), lambda i, j: (i, 0)),
            pl.BlockSpec((k, tn), lambda i, j: (0, j)),
        ],
        out_specs=pl.BlockSpec((tm, tn), lambda i, j: (i, j)),
        out_shape=jax.ShapeDtypeStruct((m, n), out_dtype),
        compiler_params=_cparams("parallel", "parallel"),
        name="cast_mm",
    )(x, w)


def _norm_mm_kernel(x_ref, g_ref, w_ref, o_ref, xn_ref):
    @pl.when(pl.program_id(1) == 0)
    def _():
        xn_ref[...] = _rms(x_ref[...], g_ref[...]).astype(BF16)

    o_ref[...] = _dot(xn_ref[...], w_ref[...]).astype(o_ref.dtype)


def _norm_mm(x, g, w, out_dtype, tm, tn):
    m, d = x.shape
    n = w.shape[1]
    return pl.pallas_call(
        _norm_mm_kernel,
        grid=(m // tm, n // tn),
        in_specs=[
            pl.BlockSpec((tm, d), lambda i, j: (i, 0)),
            pl.BlockSpec((1, d), lambda i, j: (0, 0)),
            pl.BlockSpec((d, tn), lambda i, j: (0, j)),
        ],
        out_specs=pl.BlockSpec((tm, tn), lambda i, j: (i, j)),
        out_shape=jax.ShapeDtypeStruct((m, n), out_dtype),
        scratch_shapes=[pltpu.VMEM((tm, d), BF16)],
        compiler_params=_cparams("parallel", "arbitrary"),
        name="norm_mm",
    )(x, g, w)


def _mm_norm_res_kernel(a_ref, w_ref, x_ref, g_ref, o_ref, *, nk):
    if nk == 1:
        o_ref[...] = x_ref[...] + _rms(_dot(a_ref[...], w_ref[...]), g_ref[...])
        return
    k = pl.program_id(1)

    @pl.when(k == 0)
    def _():
        o_ref[...] = jnp.zeros(o_ref.shape, F32)

    o_ref[...] += _dot(a_ref[...], w_ref[...])

    @pl.when(k == nk - 1)
    def _():
        o_ref[...] = x_ref[...] + _rms(o_ref[...], g_ref[...])


def _mm_norm_res(a, w, x, g, tm, tk):
    m, kdim = a.shape
    d = w.shape[1]
    nk = kdim // tk
    return pl.pallas_call(
        functools.partial(_mm_norm_res_kernel, nk=nk),
        grid=(m // tm, nk),
        in_specs=[
            pl.BlockSpec((tm, tk), lambda i, k: (i, k)),
            pl.BlockSpec((tk, d), lambda i, k: (k, 0)),
            pl.BlockSpec((tm, d), lambda i, k: (i, 0)),
            pl.BlockSpec((1, d), lambda i, k: (0, 0)),
        ],
        out_specs=pl.BlockSpec((tm, d), lambda i, k: (i, 0)),
        out_shape=jax.ShapeDtypeStruct((m, d), F32),
        compiler_params=_cparams("parallel", "arbitrary"),
        name="mm_norm_res",
    )(a, w, x, g)


def _ffn_kernel(x_ref, g_in_ref, g_out_ref, wg_ref, wu_ref, wo_ref, o_ref, xn_ref, *, nf):
    j = pl.program_id(1)

    @pl.when(j == 0)
    def _():
        xn_ref[...] = _rms(x_ref[...], g_in_ref[...]).astype(BF16)
        o_ref[...] = jnp.zeros(o_ref.shape, F32)

    xn = xn_ref[...]
    tf = wo_ref.shape[0]
    half = tf // 2 if tf % (2 * LANES) == 0 else tf
    for lo in range(0, tf, half):
        gate = _dot(xn, wg_ref[:, lo:lo + half])
        up = _dot(xn, wu_ref[:, lo:lo + half])
        act = (gate * jax.nn.sigmoid(gate) * up).astype(BF16)
        o_ref[...] += _dot(act, wo_ref[lo:lo + half, :])

    @pl.when(j == nf - 1)
    def _():
        o_ref[...] = x_ref[...] + _rms(o_ref[...], g_out_ref[...])


def _ffn(x, g_in, g_out, w_in, w_out, layer, tm, tf):
    m, d = x.shape
    f = w_out.shape[1]
    nf = f // tf
    return pl.pallas_call(
        functools.partial(_ffn_kernel, nf=nf),
        grid=(m // tm, nf),
        in_specs=[
            pl.BlockSpec((tm, d), lambda i, j: (i, 0)),
            pl.BlockSpec((1, d), lambda i, j: (0, 0)),
            pl.BlockSpec((1, d), lambda i, j: (0, 0)),
            pl.BlockSpec((None, d, tf), lambda i, j: (layer, 0, j)),
            pl.BlockSpec((None, d, tf), lambda i, j: (layer, 0, j + nf)),
            pl.BlockSpec((None, tf, d), lambda i, j: (layer, j, 0)),
        ],
        out_specs=pl.BlockSpec((tm, d), lambda i, j: (i, 0)),
        out_shape=jax.ShapeDtypeStruct((m, d), F32),
        scratch_shapes=[pltpu.VMEM((tm, d), BF16)],
        compiler_params=_cparams("parallel", "arbitrary"),
        name="ffn",
    )(x, g_in, g_out, w_in, w_in, w_out)


def _flash_kernel(q_ref, k_ref, v_ref, o_ref, m_ref, l_ref, acc_ref, *, tq):
    qi = pl.program_id(2)
    heads = FLASH_HEADS
    qts = [q_ref[:, h * HEAD_PAD:(h + 1) * HEAD_PAD].astype(F32).T.astype(BF16) for h in range(heads)]
    m_ref[...] = jnp.full(m_ref.shape, -jnp.inf, F32)
    l_ref[...] = jnp.zeros(l_ref.shape, F32)
    acc_ref[...] = jnp.zeros(acc_ref.shape, F32)

    def step(kc, diagonal):
        start = pl.multiple_of(kc * tq, tq)
        scores = [_dot(k_ref[pl.ds(start, tq), h * HEAD_PAD:(h + 1) * HEAD_PAD], qts[h]) for h in range(heads)]
        for h in range(heads):
            v = v_ref[pl.ds(start, tq), h * V_DIM:(h + 1) * V_DIM]
            s = scores[h]
            if diagonal:
                key = lax.broadcasted_iota(jnp.int32, s.shape, 0)
                query = lax.broadcasted_iota(jnp.int32, s.shape, 1)
                s = jnp.where(key <= query, s, NEG_INF)
            m_prev = m_ref[h]
            m_new = jnp.maximum(m_prev, jnp.max(s, axis=0, keepdims=True))
            alpha = jnp.exp2(m_prev - m_new)
            p = jnp.exp2(s - m_new)
            l_ref[h] = alpha * l_ref[h] + jnp.sum(p, axis=0, keepdims=True)
            acc_ref[h] = alpha * acc_ref[h] + _dot_tn(v, p.astype(BF16))
            m_ref[h] = m_new

    def body(kc, carry):
        step(kc, False)
        return carry

    lax.fori_loop(0, qi, body, 0)
    step(qi, True)
    for h in range(heads):
        o_ref[:, h * V_DIM:(h + 1) * V_DIM] = (acc_ref[h] / l_ref[h]).T.astype(o_ref.dtype)


def _mla_prompt_attention(q, kv, batch, seq, tq):
    nq = seq // tq
    hb = FLASH_HEADS
    v_block0 = MLA_HEADS * HEAD_PAD // (hb * V_DIM)
    return pl.pallas_call(
        functools.partial(_flash_kernel, tq=tq),
        grid=(batch, MLA_HEADS // hb, nq),
        in_specs=[
            pl.BlockSpec((tq, hb * HEAD_PAD), lambda b, h, i: (b * nq + i, h)),
            pl.BlockSpec((seq, hb * HEAD_PAD), lambda b, h, i: (b, h)),
            pl.BlockSpec((seq, hb * V_DIM), lambda b, h, i: (b, v_block0 + h)),
        ],
        out_specs=pl.BlockSpec((tq, hb * V_DIM), lambda b, h, i: (b * nq + i, h)),
        out_shape=jax.ShapeDtypeStruct((batch * seq, MLA_HEADS * V_DIM), BF16),
        scratch_shapes=[
            pltpu.VMEM((hb, 1, tq), F32),
            pltpu.VMEM((hb, 1, tq), F32),
            pltpu.VMEM((hb, V_DIM, tq), F32),
        ],
        compiler_params=_cparams("parallel", "parallel", "arbitrary"),
        name="mla_prompt_attention",
    )(q, kv, kv)


def _head_mm_kernel(a_ref, w_ref, o_ref):
    o_ref[...] = _dot(a_ref[...], w_ref[...]).astype(o_ref.dtype)


def _q_absorb(q_s, w_uk_t):
    db = q_s.shape[0]
    return pl.pallas_call(
        _head_mm_kernel,
        grid=(MLA_HEADS,),
        in_specs=[
            pl.BlockSpec((db, NOPE_DIM), lambda h: (0, 2 * h)),
            pl.BlockSpec((None, NOPE_DIM, KV_LORA), lambda h: (h, 0, 0)),
        ],
        out_specs=pl.BlockSpec((None, db, KV_LORA), lambda h: (h, 0, 0)),
        out_shape=jax.ShapeDtypeStruct((MLA_HEADS, db, KV_LORA), BF16),
        compiler_params=_cparams("parallel"),
        name="mla_q_absorb",
    )(q_s, w_uk_t)


def _v_up(o_lat_t, w_uv_t):
    db = o_lat_t.shape[1]
    return pl.pallas_call(
        _head_mm_kernel,
        grid=(MLA_HEADS,),
        in_specs=[
            pl.BlockSpec((None, db, KV_LORA), lambda h: (h, 0, 0)),
            pl.BlockSpec((None, KV_LORA, V_DIM), lambda h: (h, 0, 0)),
        ],
        out_specs=pl.BlockSpec((db, V_DIM), lambda h: (0, h)),
        out_shape=jax.ShapeDtypeStruct((db, MLA_HEADS * V_DIM), BF16),
        compiler_params=_cparams("parallel"),
        name="mla_v_up",
    )(o_lat_t, w_uv_t)


def _mla_decode_kernel(pt_ref, q_ref, new_ref, cache_ref, o_ref, pages_ref, kvt0_ref, kvt1_ref, s0_ref, s1_ref,
                       m_ref, l_ref, acc_ref, sem, *, n_pages, n_chunks, n_steps):
    kvt_refs = (kvt0_ref, kvt1_ref)
    s_refs = (s0_ref, s1_ref)
    t = pl.program_id(0)
    prev_chunk = (t + n_chunks - 1) % n_chunks

    def page_copy(step, p):
        chunk = jnp.minimum(step, n_steps - 2)
        slot = step % DEC_SLOTS
        return pltpu.make_async_copy(cache_ref.at[pt_ref[chunk * n_pages + p]], pages_ref.at[slot, p], sem.at[slot])

    def start_chunk(step):
        for p in range(n_pages):
            page_copy(step, p).start(priority=p % 2)

    def wait_chunk(step):
        slot = step % DEC_SLOTS
        pltpu.make_async_copy(cache_ref.at[pl.ds(0, n_pages)], pages_ref.at[slot], sem.at[slot]).wait()

    @pl.when(t == 0)
    def _():
        for step in range(DEC_SLOTS - 1):
            start_chunk(step)
        kvt_refs[1][...] = jnp.zeros(kvt_refs[1].shape, BF16)
        s_refs[1][...] = jnp.zeros(s_refs[1].shape, F32)
        m_ref[...] = jnp.zeros(m_ref.shape, F32)
        l_ref[...] = jnp.zeros(l_ref.shape, F32)
        acc_ref[...] = jnp.zeros(acc_ref.shape, F32)

    @pl.when(jnp.logical_and(t > 0, prev_chunk == 0))
    def _():
        new = new_ref[...].astype(BF16).astype(F32)
        m_ref[...] = jnp.sum(q_ref[...].astype(F32) * new, axis=-1, keepdims=True) * MLA_SCALE
        l_ref[...] = jnp.ones(l_ref.shape, F32)
        acc_ref[...] = jnp.broadcast_to(new, acc_ref.shape)

    def work(cur, prev):
        kvt_cur, kvt_prev = kvt_refs[cur], kvt_refs[prev]
        slot = t % DEC_SLOTS
        start_chunk(t + DEC_SLOTS - 1)
        wait_chunk(t)
        for p in range(n_pages):
            kvt_cur[:, p * PAGE_SIZE:(p + 1) * PAGE_SIZE] = pages_ref[slot, p].astype(BF16)
        s_refs[cur][...] = _dot(q_ref[...], kvt_cur[...]) * MLA_SCALE
        s = s_refs[prev][...]
        m_prev = m_ref[...]
        m_new = jnp.maximum(m_prev, jnp.max(s, axis=-1, keepdims=True))
        alpha = jnp.exp(m_prev - m_new)
        p = jnp.exp(s - m_new)
        l_ref[...] = alpha * l_ref[...] + jnp.sum(p, axis=-1, keepdims=True)
        acc_ref[...] = alpha * acc_ref[...] + _dot_nt(p.astype(BF16), kvt_prev[...])
        m_ref[...] = m_new

    @pl.when(t % 2 == 0)
    def _():
        work(0, 1)

    @pl.when(t % 2 == 1)
    def _():
        work(1, 0)

    @pl.when(jnp.logical_and(t > 0, prev_chunk == n_chunks - 1))
    def _():
        o_ref[...] = (acc_ref[...] / l_ref[...])[:, :KV_LORA]

    @pl.when(t == n_steps - 1)
    def _():
        for step in range(n_steps, n_steps + DEC_SLOTS - 1):
            wait_chunk(step)


def _mla_decode(q_cat, rows_new, cache_t, page_table):
    db = q_cat.shape[0]
    n_tab = page_table.shape[1]
    n_pages = min(DEC_PAGES, n_tab // 2)
    n_chunks = n_tab // n_pages
    assert n_chunks >= 2 and n_chunks * n_pages == n_tab
    n_steps = db * n_chunks + 1
    width = n_pages * PAGE_SIZE
    chunk = lambda t: jnp.minimum(t, n_steps - 2)
    batch_of = lambda t: chunk(t) // n_chunks

    grid_spec = pltpu.PrefetchScalarGridSpec(
        num_scalar_prefetch=1,
        grid=(n_steps,),
        in_specs=[
            pl.BlockSpec((None, MLA_HEADS, ROW_DIM), lambda t, pt: (batch_of(t), 0, 0)),
            pl.BlockSpec((None, 1, ROW_DIM), lambda t, pt: (batch_of(t), 0, 0)),
            pl.BlockSpec(memory_space=pl.ANY),
        ],
        out_specs=pl.BlockSpec((None, MLA_HEADS, KV_LORA),
                               lambda t, pt: (jnp.maximum(t - 1, 0) // n_chunks, 0, 0)),
        scratch_shapes=[
            pltpu.VMEM((DEC_SLOTS, n_pages, ROW_DIM, PAGE_SIZE), F32),
            pltpu.VMEM((ROW_DIM, width), BF16),
            pltpu.VMEM((ROW_DIM, width), BF16),
            pltpu.VMEM((MLA_HEADS, width), F32),
            pltpu.VMEM((MLA_HEADS, width), F32),
            pltpu.VMEM((MLA_HEADS, 1), F32),
            pltpu.VMEM((MLA_HEADS, 1), F32),
            pltpu.VMEM((MLA_HEADS, ROW_DIM), F32),
            pltpu.SemaphoreType.DMA((DEC_SLOTS,)),
        ],
    )
    return pl.pallas_call(
        functools.partial(_mla_decode_kernel, n_pages=n_pages, n_chunks=n_chunks, n_steps=n_steps),
        grid_spec=grid_spec,
        out_shape=jax.ShapeDtypeStruct((db, MLA_HEADS, KV_LORA), F32),
        compiler_params=_cparams("arbitrary"),
        name="mla_decode",
    )(page_table.reshape(-1), q_cat, rows_new.reshape(db, 1, ROW_DIM), cache_t)


def _dil_proj_kernel(x_ref, gq_ref, gkv_ref, wq_ref, wk_ref, wv_ref, qkv_ref, kv_ref, y_ref, *, dil):
    x = x_ref[...]
    xr = x * lax.rsqrt(jnp.mean(x * x, axis=-1, keepdims=True) + EPS)
    xq = (xr * gq_ref[...]).astype(BF16)
    xkv = (xr * gkv_ref[...]).astype(BF16)
    tm = x.shape[0]
    n = tm // dil
    tw = kv_ref.shape[0]
    for j, w_ref in enumerate((wq_ref, wk_ref, wv_ref)):
        y = _dot(xq if j == 0 else xkv, w_ref[...])
        if j > 0:
            kv_ref[:, (j - 1) * GROUP_DIM:j * GROUP_DIM] = y[tm - tw:, :]
        if dil == 1:
            qkv_ref[:, j * GROUP_DIM:(j + 1) * GROUP_DIM] = y.astype(BF16)
            continue
        for h in range(GROUP_HEADS):
            y_ref[h] = y[:, h * B_HEAD_DIM:(h + 1) * B_HEAD_DIM]
        for r in range(dil):
            for h in range(GROUP_HEADS):
                lo = r * QKV_DIM + j * GROUP_DIM + h * B_HEAD_DIM
                qkv_ref[:, lo:lo + B_HEAD_DIM] = y_ref[h, pl.ds(r, n, stride=dil), :].astype(BF16)


def _dil_proj(x, gq, gkv, w_q, w_kv, g, dil, window, seq, tm):
    m, d = x.shape
    tiles_per_seq = seq // tm
    tw = min(tm, window)
    win_tiles = window // tw
    assert window <= seq and window % tw == 0 and tm % tw == 0

    def win_block(i):
        return (i // tiles_per_seq) * win_tiles + jnp.maximum(i % tiles_per_seq - (tiles_per_seq - win_tiles), 0)

    w_block = (d, GROUP_DIM)
    return pl.pallas_call(
        functools.partial(_dil_proj_kernel, dil=dil),
        grid=(m // tm,),
        in_specs=[
            pl.BlockSpec((tm, d), lambda i: (i, 0)),
            pl.BlockSpec((1, d), lambda i: (0, 0)),
            pl.BlockSpec((1, d), lambda i: (0, 0)),
            pl.BlockSpec(w_block, lambda i: (0, g)),
            pl.BlockSpec(w_block, lambda i: (0, g)),
            pl.BlockSpec(w_block, lambda i: (0, N_GROUPS + g)),
        ],
        out_specs=[
            pl.BlockSpec((tm // dil, dil * QKV_DIM), lambda i: (i, 0)),
            pl.BlockSpec((tw, 2 * GROUP_DIM), lambda i: (win_block(i), 0)),
        ],
        out_shape=[
            jax.ShapeDtypeStruct((m // dil, dil * QKV_DIM), BF16),
            jax.ShapeDtypeStruct((m // seq * window, 2 * GROUP_DIM), F32),
        ],
        scratch_shapes=[pltpu.VMEM((GROUP_HEADS, tm, B_HEAD_DIM), F32)],
        compiler_params=_cparams("arbitrary"),
        name="dilated_proj",
    )(x, gq, gkv, w_q, w_kv, w_kv)


def _dil_prompt_kernel(q_ref, kc_ref, kp_ref, vc_ref, vp_ref, bias_ref, o_ref, lse_ref):
    first = pl.program_id(2) == 0
    heads = [slice(h * B_HEAD_DIM, (h + 1) * B_HEAD_DIM) for h in range(GROUP_HEADS)]
    scores = [_dot_nt(q_ref[:, hs], jnp.concatenate([kp_ref[:, hs], kc_ref[:, hs]], axis=0)) for hs in heads]
    for h, hs in enumerate(heads):
        v = jnp.concatenate([vp_ref[:, hs], vc_ref[:, hs]], axis=0)
        s = scores[h] * B_SCALE + bias_ref[h]
        col = lax.broadcasted_iota(jnp.int32, s.shape, 1)
        s = jnp.where(jnp.logical_and(first, col < DIL_TILE), NEG_INF, s)
        m = jnp.max(s, axis=-1, keepdims=True)
        p = jnp.exp(s - m)
        l = jnp.sum(p, axis=-1, keepdims=True)
        o_ref[:, hs] = _dot(p.astype(BF16), v) / l
        lse_ref[:, hs] = jnp.broadcast_to(m + jnp.log(l), (DIL_TILE, B_HEAD_DIM))


def _dil_prompt(qkv, bias, dil, batch, seq):
    sd = seq // dil
    nq = sd // DIL_TILE
    blk = (DIL_TILE, GROUP_DIM)
    cur = lambda b, i: b * nq + i
    prev = lambda b, i: b * nq + jnp.maximum(i - 1, 0)
    return pl.pallas_call(
        _dil_prompt_kernel,
        grid=(batch, dil, nq),
        in_specs=[
            pl.BlockSpec(blk, lambda b, r, i: (cur(b, i), 3 * r)),
            pl.BlockSpec(blk, lambda b, r, i: (cur(b, i), 3 * r + 1)),
            pl.BlockSpec(blk, lambda b, r, i: (prev(b, i), 3 * r + 1)),
            pl.BlockSpec(blk, lambda b, r, i: (cur(b, i), 3 * r + 2)),
            pl.BlockSpec(blk, lambda b, r, i: (prev(b, i), 3 * r + 2)),
            pl.BlockSpec((GROUP_HEADS, DIL_TILE, 2 * DIL_TILE), lambda b, r, i: (0, 0, 0)),
        ],
        out_specs=[
            pl.BlockSpec(blk, lambda b, r, i: (cur(b, i), r)),
            pl.BlockSpec(blk, lambda b, r, i: (cur(b, i), r)),
        ],
        out_shape=[
            jax.ShapeDtypeStruct((batch * sd, dil * GROUP_DIM), F32),
            jax.ShapeDtypeStruct((batch * sd, dil * GROUP_DIM), F32),
        ],
        compiler_params=_cparams("parallel", "parallel", "arbitrary"),
        name="dilated_prompt",
    )(qkv, qkv, qkv, qkv, qkv, bias)


def _dil_sample_kernel(q_ref, kn_ref, vn_ref, buf_ref, bias_ref, o_ref, lse_ref, *, dil):
    row = lax.broadcasted_iota(jnp.int32, (SUBLANES, B_HEAD_DIM), 0)
    row_wide = lax.broadcasted_iota(jnp.int32, (SUBLANES, GROUP_DIM), 0)
    own = row_wide == lax.broadcasted_iota(jnp.int32, (SUBLANES, GROUP_DIM), 1) // B_HEAD_DIM
    bias_keys = bias_ref[:, :DIL_KEYS]
    bias_new = bias_ref[:, DIL_KEYS:DIL_KEYS + 1]

    def cached(b, r):
        if dil == 1:
            return buf_ref[b, pl.ds(r, DIL_KEYS, stride=BUF_ROWS), :]
        return buf_ref[b, :, r, :]

    for b in range(DIL_SAMPLE_BLOCK):
        qb = q_ref[b:b + 1, :]
        s = jnp.zeros((SUBLANES, DIL_KEYS), F32)
        for h in range(GROUP_HEADS):
            qh = jnp.broadcast_to(qb[:, h * B_HEAD_DIM:(h + 1) * B_HEAD_DIM], (SUBLANES, B_HEAD_DIM))
            sh = _dot_nt(qh.astype(BF16), cached(b, 2 * h).astype(BF16))
            s = jnp.where(row == h, sh, s)
        s = s * B_SCALE + bias_keys
        kn = kn_ref[b:b + 1, :].astype(BF16).astype(F32)
        vn = vn_ref[b:b + 1, :].astype(BF16).astype(F32)
        q_own = jnp.where(own, qb, 0.0).astype(BF16).astype(F32)
        s_new = jnp.sum(q_own * kn, axis=-1, keepdims=True) * B_SCALE + bias_new
        m = jnp.maximum(jnp.max(s, axis=-1, keepdims=True), s_new)
        p = jnp.exp(s - m)
        p_new = jnp.exp(s_new - m)
        l = jnp.sum(p, axis=-1, keepdims=True) + p_new
        p16 = p.astype(BF16)
        pn = p_new.astype(BF16).astype(F32)
        lse = m + jnp.log(l)
        for h in range(GROUP_HEADS):
            hs = slice(h * B_HEAD_DIM, (h + 1) * B_HEAD_DIM)
            oh = (_dot(p16, cached(b, 2 * h + 1).astype(BF16)) + pn * vn[:, hs]) / l
            o_ref[b:b + 1, hs] = jnp.sum(jnp.where(row == h, oh, 0.0), axis=0, keepdims=True)
            lse_ref[b:b + 1, hs] = jnp.sum(jnp.where(row == h, lse, 0.0), axis=0, keepdims=True)


def _dil_sample(q_s, kv_s, buf, bias, g, dil):
    db, length = buf.shape[:2]
    bb = DIL_SAMPLE_BLOCK
    rows = jnp.transpose(buf, (0, 1, 3, 2, 4))
    if dil == 1:
        rows = rows.reshape(db, length * BUF_ROWS, B_HEAD_DIM)
        buf_spec = pl.BlockSpec((bb, length * BUF_ROWS, B_HEAD_DIM), lambda i: (i, 0, 0))
    else:
        rows = rows.reshape(db, length // dil, dil * BUF_ROWS, B_HEAD_DIM)
        buf_spec = pl.BlockSpec((bb, DIL_KEYS, BUF_ROWS_PAD, B_HEAD_DIM), lambda i: (i, 0, 0, 0))
    return pl.pallas_call(
        functools.partial(_dil_sample_kernel, dil=dil),
        grid=(db // bb,),
        in_specs=[
            pl.BlockSpec((bb, GROUP_DIM), lambda i: (i, g)),
            pl.BlockSpec((bb, GROUP_DIM), lambda i: (i, g)),
            pl.BlockSpec((bb, GROUP_DIM), lambda i: (i, N_GROUPS + g)),
            buf_spec,
            pl.BlockSpec((SUBLANES, 2 * DIL_KEYS), lambda i: (0, 0)),
        ],
        out_specs=[
            pl.BlockSpec((bb, GROUP_DIM), lambda i: (i, 0)),
            pl.BlockSpec((bb, GROUP_DIM), lambda i: (i, 0)),
        ],
        out_shape=[
            jax.ShapeDtypeStruct((db, GROUP_DIM), F32),
            jax.ShapeDtypeStruct((db, GROUP_DIM), F32),
        ],
        compiler_params=_cparams("parallel"),
        name="dilated_sample",
    )(q_s, kv_s, kv_s, rows, bias)


def _merge_out_kernel(o0, o1, o2, l0, l1, l2, w_ref, x_ref, g_ref, out_ref, merged_ref, *scratch, dils):
    srcs = ((o0, l0), (o1, l1), (o2, l2))
    tm = x_ref.shape[0]
    ordered = []
    si = 0
    for (o_src, l_src), dil in zip(srcs, dils):
        if dil == 1:
            ordered.append(None)
            continue
        o_dst, l_dst = scratch[si], scratch[si + 1]
        si += 2
        n = tm // dil
        for r in range(dil):
            for h in range(GROUP_HEADS):
                lo = r * GROUP_DIM + h * B_HEAD_DIM
                o_dst[h, pl.ds(r, n, stride=dil), :] = o_src[:, lo:lo + B_HEAD_DIM]
                l_dst[h, pl.ds(r, n, stride=dil), :] = l_src[:, lo:lo + B_HEAD_DIM]
        ordered.append((o_dst, l_dst))
    for h in range(GROUP_HEADS):
        hs = slice(h * B_HEAD_DIM, (h + 1) * B_HEAD_DIM)
        os_, ls_ = [], []
        for (o_src, l_src), dst in zip(srcs, ordered):
            os_.append(o_src[:, hs] if dst is None else dst[0][h])
            ls_.append(l_src[:, hs] if dst is None else dst[1][h])
        mx = jnp.maximum(jnp.maximum(ls_[0], ls_[1]), ls_[2])
        e0, e1, e2 = jnp.exp(ls_[0] - mx), jnp.exp(ls_[1] - mx), jnp.exp(ls_[2] - mx)
        merged = (e0 * os_[0] + e1 * os_[1] + e2 * os_[2]) / (e0 + e1 + e2)
        merged_ref[:, hs] = merged.astype(BF16)
    y = _dot(merged_ref[...], w_ref[...])
    out_ref[...] = x_ref[...] + _rms(y, g_ref[...])


def _merge_out(outs, lses, dils, w, x, g, tm):
    m, d = x.shape
    parts = [pl.BlockSpec((tm // dil, dil * GROUP_DIM), lambda i: (i, 0)) for dil in dils]
    scratch = [pltpu.VMEM((tm, GROUP_DIM), BF16)]
    for dil in dils:
        if dil > 1:
            scratch += [pltpu.VMEM((GROUP_HEADS, tm, B_HEAD_DIM), F32)] * 2
    return pl.pallas_call(
        functools.partial(_merge_out_kernel, dils=tuple(dils)),
        grid=(m // tm,),
        in_specs=parts + parts + [
            pl.BlockSpec((GROUP_DIM, d), lambda i: (0, 0)),
            pl.BlockSpec((tm, d), lambda i: (i, 0)),
            pl.BlockSpec((1, d), lambda i: (0, 0)),
        ],
        out_specs=pl.BlockSpec((tm, d), lambda i: (i, 0)),
        out_shape=jax.ShapeDtypeStruct((m, d), F32),
        scratch_shapes=scratch,
        compiler_params=_cparams("parallel"),
        name="dilated_merge_out",
    )(*outs, *lses, w, x, g)


def _t5_bucket(dist):
    max_exact = T5_BUCKETS // 2
    d = np.maximum(dist, 1).astype(np.float32)
    large = max_exact + (np.log(d / max_exact) / math.log(T5_MAX_DIST / max_exact)
                         * (T5_BUCKETS - max_exact)).astype(np.int32)
    large = np.minimum(large, T5_BUCKETS - 1)
    return np.where(dist < max_exact, dist, large).astype(np.int32)


def _rows_static(table, idx):
    pieces, start = [], 0
    for i in range(1, len(idx) + 1):
        if i == len(idx) or idx[i] != idx[start]:
            row = table[int(idx[start]):int(idx[start]) + 1]
            pieces.append(jnp.broadcast_to(row, (i - start,) + table.shape[1:]))
            start = i
    return jnp.concatenate(pieces, axis=0)


def _group_bias(t5_bias, g):
    window, dil = DIL_GROUPS[g]
    buckets = _t5_bucket(dil * np.arange(window // dil + 1))
    return _rows_static(t5_bias[:, g * GROUP_HEADS:(g + 1) * GROUP_HEADS].astype(F32), buckets)


def _prompt_bias(bias):
    n = 2 * DIL_TILE
    heads = bias.shape[1]
    period = jnp.concatenate([bias[::-1], jnp.full((n - DIL_KEYS, heads), NEG_INF, F32)], axis=0)
    band = jnp.tile(period, (DIL_TILE, 1))[:DIL_TILE * n].reshape(DIL_TILE, n, heads)
    return jnp.transpose(band, (2, 0, 1))


def _sample_bias(bias):
    keys = bias[1:][::-1].T
    own = jnp.broadcast_to(bias[0][:, None], (GROUP_HEADS, DIL_KEYS))
    both = jnp.concatenate([keys, own], axis=1)
    return jnp.pad(both, ((0, SUBLANES - GROUP_HEADS), (0, 0)))


def _rope_tables(pos):
    half = ROPE_DIM // 2
    inv = ROPE_THETA ** (-jnp.arange(half, dtype=F32) / half)
    ang = pos.astype(F32)[:, None] * inv
    cos, sin = jnp.cos(ang), jnp.sin(ang)
    zero = jnp.zeros((pos.shape[0], LANES - ROPE_DIM), F32)
    return (jnp.concatenate([cos, cos, zero], axis=1), jnp.concatenate([-sin, sin, zero], axis=1))


def _tile(n, pref):
    return pref if n % pref == 0 else n


def kernel(x_prompt, x_sample, cache_mla, page_table, cache_b_win0, cache_b_win1, cache_b_win2,
           norm_gains, w_mla_in, g_q_a, g_kv_a, w_uq, w_uk, w_uv, w_mla_o,
           g_kv_b, w_kv_b, w_q_b, w_b_o, t5_bias, w_ffn_in, w_ffn_out):
    batch, seq, d = x_prompt.shape
    db = x_sample.shape[0]
    assert x_sample.shape[1] == 1 and DEPTH == 2 and N_A_LAYERS == 1
    win_bufs = (cache_b_win0, cache_b_win1, cache_b_win2)
    for gi, (window, dil) in enumerate(DIL_GROUPS):
        assert win_bufs[gi].shape[1] == window and window // dil == DIL_KEYS
        assert seq % (dil * DIL_TILE) == 0
    mp = batch * seq
    xp = x_prompt.reshape(mp, d)
    xs = x_sample.reshape(db, d)
    gains = norm_gains.reshape(DEPTH, 4, 1, d)
    tm_p = _tile(mp, 1024)
    tm_h = _tile(mp, 512)

    rope_cols = jnp.pad(w_mla_in[0][:, Q_LORA + KV_LORA:], ((0, 0), (0, LANES - ROPE_DIM)))
    w_in_pad = jnp.concatenate([w_mla_in[0][:, :Q_LORA + KV_LORA], rope_cols], axis=1).astype(BF16)
    w_uq_pad = jnp.pad(w_uq[0].reshape(Q_LORA, MLA_HEADS, NOPE_DIM + ROPE_DIM),
                       ((0, 0), (0, 0), (0, HEAD_PAD - NOPE_DIM - ROPE_DIM)))
    w_uq_pad = w_uq_pad.reshape(Q_LORA, MLA_HEADS * HEAD_PAD).astype(BF16)
    w_k_lat = jnp.pad(w_uk[0], ((0, 0), (0, 0), (0, HEAD_PAD - NOPE_DIM)))
    rope_copy = jnp.pad(jnp.eye(ROPE_DIM, dtype=F32), ((0, 0), (NOPE_DIM, HEAD_PAD - NOPE_DIM - ROPE_DIM)))
    rope_copy = jnp.broadcast_to(rope_copy[:, None, :], (ROPE_DIM, MLA_HEADS, HEAD_PAD))
    w_k = jnp.concatenate([w_k_lat, rope_copy], axis=0).reshape(ROW_DIM, MLA_HEADS * HEAD_PAD)
    w_v = jnp.pad(w_uv[0].reshape(KV_LORA, MLA_HEADS * V_DIM), ((0, ROPE_DIM), (0, 0)))
    w_kv = jnp.concatenate([w_k, w_v], axis=1).astype(BF16)
    w_uk_t = jnp.transpose(w_uk[0], (1, 2, 0)).astype(BF16)
    w_uv_t = jnp.transpose(w_uv[0], (1, 0, 2)).astype(BF16)
    w_o = w_mla_o[0].astype(BF16)
    w_kvb = w_kv_b.astype(BF16)
    w_qb = w_q_b[0].astype(BF16)
    w_bo = w_b_o[0].astype(BF16)
    w_f_in = w_ffn_in.astype(BF16)
    w_f_out = w_ffn_out.astype(BF16)
    f = w_ffn_out.shape[1]
    tf = _tile(f, 512)

    c_p, s_p = _rope_tables(jnp.tile(jnp.arange(seq), batch))
    c_s, s_s = _rope_tables(jnp.full((db,), PAST_LEN, jnp.int32))

    g = gains[0]
    cq_p, rows_p = _mla_in(xp, g[0], w_in_pad, g_q_a[0:1], g_kv_a[0:1], c_p, s_p, tm_h)
    cq_s, rows_s = _mla_in(xs, g[0], w_in_pad, g_q_a[0:1], g_kv_a[0:1], c_s, s_s, db)
    q_p = _q_up(cq_p, w_uq_pad, c_p, s_p, tm_p, 1024, out_scale=MLA_SCALE * math.log2(math.e))
    q_s = _q_up(cq_s, w_uq_pad, c_s, s_s, db, 1024)

    kv_p = _cast_mm(rows_p, w_kv, BF16, tm_p, 2048)
    attn_p = _mla_prompt_attention(q_p, kv_p, batch, seq, _tile(seq, 512))
    xp = _mm_norm_res(attn_p, w_o, xp, g[1], tm_h, w_o.shape[0])

    q_lat = _q_absorb(q_s, w_uk_t)
    q_rope_s = q_s.reshape(db, MLA_HEADS, HEAD_PAD)[:, :, NOPE_DIM:NOPE_DIM + ROPE_DIM]
    q_cat = jnp.concatenate([jnp.transpose(q_lat, (1, 0, 2)), q_rope_s], axis=-1)
    cache_t = jnp.swapaxes(cache_mla[0], 1, 2)
    o_lat = _mla_decode(q_cat, rows_s, cache_t, page_table)
    attn_s = _v_up(jnp.transpose(o_lat, (1, 0, 2)).astype(BF16), w_uv_t)
    xs = _mm_norm_res(attn_s, w_o, xs, g[1], db, 512)

    xp = _ffn(xp, g[2], g[3], w_f_in, w_f_out, 0, tm_h, tf)
    xs = _ffn(xs, g[2], g[3], w_f_in, w_f_out, 0, db, tf)

    g = gains[1]
    g_kvb = g_kv_b.reshape(1, d)
    kvb_s = _norm_mm(xs, g_kvb, w_kvb, F32, db, 512)
    qb_s = _norm_mm(xs, g[0], w_qb, F32, db, GROUP_DIM)
    outs_p, lses_p, outs_s, lses_s, kv_groups = [], [], [], [], []
    dils = [dil for _, dil in DIL_GROUPS]
    biases = [_group_bias(t5_bias, gi) for gi in range(N_GROUPS)]
    band_all = _prompt_bias(jnp.concatenate(biases, axis=1))
    for gi, (window, dil) in enumerate(DIL_GROUPS):
        qkv, kv_g = _dil_proj(xp, g[0], g_kvb, w_qb, w_kvb, gi, dil, window, seq, tm_h)
        kv_groups.append(kv_g)
        bias = biases[gi]
        o, lse = _dil_prompt(qkv, band_all[gi * GROUP_HEADS:(gi + 1) * GROUP_HEADS], dil, batch, seq)
        outs_p.append(o)
        lses_p.append(lse)
        o, lse = _dil_sample(qb_s, kvb_s, win_bufs[gi], _sample_bias(bias), gi, dil)
        outs_s.append(o)
        lses_s.append(lse)
    xp = _merge_out(outs_p, lses_p, dils, w_bo, xp, g[1], _tile(mp, 256))
    xs = _merge_out(outs_s, lses_s, [1] * N_GROUPS, w_bo, xs, g[1], db)

    xp = _ffn(xp, g[2], g[3], w_f_in, w_f_out, 1, tm_h, tf)
    xs = _ffn(xs, g[2], g[3], w_f_in, w_f_out, 1, db, tf)

    kv6_s = kvb_s.reshape(db, 1, 2, N_GROUPS, GROUP_HEADS, B_HEAD_DIM)
    wins = []
    for gi, (window, _) in enumerate(DIL_GROUPS):
        wins.append(kv_groups[gi].reshape(batch, window, 2, GROUP_HEADS, B_HEAD_DIM))
        wins.append(kv6_s[:, :, :, gi])
    return (xp.reshape(batch, seq, d), xs.reshape(db, 1, d),
            rows_p.reshape(1, batch, seq, ROW_DIM), rows_s.reshape(1, db, 1, ROW_DIM), *wins)
```

```python
import functools
import math

import numpy as np
import jax
import jax.numpy as jnp
from jax import lax
from jax.experimental import pallas as pl
from jax.experimental.pallas import tpu as pltpu

D_MODEL = 2048
DEPTH = 2
PAST_LEN = 16384
PAGE_SIZE = 128
N_A_LAYERS = DEPTH // 2
MLA_HEADS = 16
Q_LORA = 512
KV_LORA = 512
NOPE_DIM = 128
ROPE_DIM = 64
V_DIM = 128
ROPE_THETA = 10000.0
MLA_SCALE = (NOPE_DIM + ROPE_DIM) ** -0.5
DIL_GROUPS = ((128, 1), (512, 4), (2048, 16))
N_GROUPS = len(DIL_GROUPS)
GROUP_HEADS = 6
B_HEAD_DIM = 128
B_HEADS = N_GROUPS * GROUP_HEADS
B_SCALE = B_HEAD_DIM ** -0.5
T5_BUCKETS = 32
T5_MAX_DIST = 2048
EPS = 1e-6
NEG_INF = -1e30

F32 = jnp.float32
BF16 = jnp.bfloat16

VMEM_LIMIT_BYTES = 56 * 1024 * 1024
LANES = 128
SUBLANES = 8
HEAD_PAD = 2 * LANES
ROW_DIM = KV_LORA + ROPE_DIM
GROUP_DIM = GROUP_HEADS * B_HEAD_DIM
QKV_DIM = 3 * GROUP_DIM
DIL_KEYS = 128
DIL_TILE = 128
FLASH_HEADS = 4
DEC_PAGES = 32
DEC_SLOTS = 3
DIL_SAMPLE_BLOCK = 8
BUF_ROWS = 2 * GROUP_HEADS
BUF_ROWS_PAD = 16


def _cparams(*semantics):
    return pltpu.CompilerParams(dimension_semantics=semantics, vmem_limit_bytes=VMEM_LIMIT_BYTES)


def _rms(x, g):
    return x * lax.rsqrt(jnp.mean(x * x, axis=-1, keepdims=True) + EPS) * g


def _dot(a, b):
    return jnp.dot(a, b, preferred_element_type=F32)


def _dot_nt(a, b):
    return lax.dot_general(a, b, (((1,), (1,)), ((), ())), preferred_element_type=F32)


def _dot_tn(a, b):
    return lax.dot_general(a, b, (((0,), (0,)), ((), ())), preferred_element_type=F32)


def _rope128(x, c, s):
    lane = lax.broadcasted_iota(jnp.int32, x.shape, 1)
    half = ROPE_DIM // 2
    swapped = jnp.where(lane < half, pltpu.roll(x, LANES - half, 1), pltpu.roll(x, half, 1))
    return x * c + swapped * s


def _mla_in_kernel(x_ref, g_ref, w_ref, gq_ref, gkv_ref, c_ref, s_ref, cq_ref, rows_ref):
    xn = _rms(x_ref[...], g_ref[...]).astype(BF16)
    a = _dot(xn, w_ref[...])
    cq_ref[...] = _rms(a[:, :Q_LORA], gq_ref[...]).astype(BF16)
    rows_ref[:, :KV_LORA] = _rms(a[:, Q_LORA:Q_LORA + KV_LORA], gkv_ref[...])
    kr = _rope128(a[:, Q_LORA + KV_LORA:], c_ref[...], s_ref[...])
    rows_ref[:, KV_LORA:] = kr[:, :ROPE_DIM]


def _mla_in(x, g, w_pad, gq, gkv, c_tab, s_tab, tm):
    m, d = x.shape
    n = w_pad.shape[1]
    return pl.pallas_call(
        _mla_in_kernel,
        grid=(m // tm,),
        in_specs=[
            pl.BlockSpec((tm, d), lambda i: (i, 0)),
            pl.BlockSpec((1, d), lambda i: (0, 0)),
            pl.BlockSpec((d, n), lambda i: (0, 0)),
            pl.BlockSpec((1, Q_LORA), lambda i: (0, 0)),
            pl.BlockSpec((1, KV_LORA), lambda i: (0, 0)),
            pl.BlockSpec((tm, LANES), lambda i: (i, 0)),
            pl.BlockSpec((tm, LANES), lambda i: (i, 0)),
        ],
        out_specs=[
            pl.BlockSpec((tm, Q_LORA), lambda i: (i, 0)),
            pl.BlockSpec((tm, ROW_DIM), lambda i: (i, 0)),
        ],
        out_shape=[
            jax.ShapeDtypeStruct((m, Q_LORA), BF16),
            jax.ShapeDtypeStruct((m, ROW_DIM), F32),
        ],
        compiler_params=_cparams("parallel"),
        name="mla_in",
    )(x, g, w_pad, gq, gkv, c_tab, s_tab)


def _q_up_kernel(cq_ref, w_ref, c_ref, s_ref, o_ref, *, heads_per_tile, out_scale):
    a = _dot(cq_ref[...], w_ref[...])
    if out_scale != 1.0:
        a = a * out_scale
    c = c_ref[...]
    s = s_ref[...]
    for h in range(heads_per_tile):
        lo = h * HEAD_PAD
        o_ref[:, lo:lo + LANES] = a[:, lo:lo + LANES].astype(BF16)
        o_ref[:, lo + LANES:lo + HEAD_PAD] = _rope128(a[:, lo + LANES:lo + HEAD_PAD], c, s).astype(BF16)


def _q_up(cq, w_pad, c_tab, s_tab, tm, tn, out_scale=1.0):
    m, k = cq.shape
    n = w_pad.shape[1]
    return pl.pallas_call(
        functools.partial(_q_up_kernel, heads_per_tile=tn // HEAD_PAD, out_scale=out_scale),
        grid=(m // tm, n // tn),
        in_specs=[
            pl.BlockSpec((tm, k), lambda i, j: (i, 0)),
            pl.BlockSpec((k, tn), lambda i, j: (0, j)),
            pl.BlockSpec((tm, LANES), lambda i, j: (i, 0)),
            pl.BlockSpec((tm, LANES), lambda i, j: (i, 0)),
        ],
        out_specs=pl.BlockSpec((tm, tn), lambda i, j: (i, j)),
        out_shape=jax.ShapeDtypeStruct((m, n), BF16),
        compiler_params=_cparams("parallel", "parallel"),
        name="mla_q_up",
    )(cq, w_pad, c_tab, s_tab)


def _cast_mm_kernel(x_ref, w_ref, o_ref):
    o_ref[...] = _dot(x_ref[...].astype(BF16), w_ref[...]).astype(o_ref.dtype)


def _cast_mm(x, w, out_dtype, tm, tn):
    m, k = x.shape
    n = w.shape[1]
    return pl.pallas_call(
        _cast_mm_kernel,
        grid=(m // tm, n // tn),
        in_specs=[
            pl.BlockSpec((tm, k), lambda i, j: (i, 0)),
            pl.BlockSpec((k, tn), lambda i, j: (0, j)),
        ],
        out_specs=pl.BlockSpec((tm, tn), lambda i, j: (i, j)),
        out_shape=jax.ShapeDtypeStruct((m, n), out_dtype),
        compiler_params=_cparams("parallel", "parallel"),
        name="cast_mm",
    )(x, w)


def _norm_mm_kernel(x_ref, g_ref, w_ref, o_ref, xn_ref):
    @pl.when(pl.program_id(1) == 0)
    def _():
        xn_ref[...] = _rms(x_ref[...], g_ref[...]).astype(BF16)

    o_ref[...] = _dot(xn_ref[...], w_ref[...]).astype(o_ref.dtype)


def _norm_mm(x, g, w, out_dtype, tm, tn):
    m, d = x.shape
    n = w.shape[1]
    return pl.pallas_call(
        _norm_mm_kernel,
        grid=(m // tm, n // tn),
        in_specs=[
            pl.BlockSpec((tm, d), lambda i, j: (i, 0)),
            pl.BlockSpec((1, d), lambda i, j: (0, 0)),
            pl.BlockSpec((d, tn), lambda i, j: (0, j)),
        ],
        out_specs=pl.BlockSpec((tm, tn), lambda i, j: (i, j)),
        out_shape=jax.ShapeDtypeStruct((m, n), out_dtype),
        scratch_shapes=[pltpu.VMEM((tm, d), BF16)],
        compiler_params=_cparams("parallel", "arbitrary"),
        name="norm_mm",
    )(x, g, w)


def _mm_norm_res_kernel(a_ref, w_ref, x_ref, g_ref, o_ref, *, nk):
    if nk == 1:
        o_ref[...] = x_ref[...] + _rms(_dot(a_ref[...], w_ref[...]), g_ref[...])
        return
    k = pl.program_id(1)

    @pl.when(k == 0)
    def _():
        o_ref[...] = jnp.zeros(o_ref.shape, F32)

    o_ref[...] += _dot(a_ref[...], w_ref[...])

    @pl.when(k == nk - 1)
    def _():
        o_ref[...] = x_ref[...] + _rms(o_ref[...], g_ref[...])


def _mm_norm_res(a, w, x, g, tm, tk):
    m, kdim = a.shape
    d = w.shape[1]
    nk = kdim // tk
    return pl.pallas_call(
        functools.partial(_mm_norm_res_kernel, nk=nk),
        grid=(m // tm, nk),
        in_specs=[
            pl.BlockSpec((tm, tk), lambda i, k: (i, k)),
            pl.BlockSpec((tk, d), lambda i, k: (k, 0)),
            pl.BlockSpec((tm, d), lambda i, k: (i, 0)),
            pl.BlockSpec((1, d), lambda i, k: (0, 0)),
        ],
        out_specs=pl.BlockSpec((tm, d), lambda i, k: (i, 0)),
        out_shape=jax.ShapeDtypeStruct((m, d), F32),
        compiler_params=_cparams("parallel", "arbitrary"),
        name="mm_norm_res",
    )(a, w, x, g)


def _ffn_kernel(x_ref, g_in_ref, g_out_ref, wg_ref, wu_ref, wo_ref, o_ref, xn_ref, *, nf):
    j = pl.program_id(1)

    @pl.when(j == 0)
    def _():
        xn_ref[...] = _rms(x_ref[...], g_in_ref[...]).astype(BF16)
        o_ref[...] = jnp.zeros(o_ref.shape, F32)

    xn = xn_ref[...]
    tf = wo_ref.shape[0]
    half = tf // 2 if tf % (2 * LANES) == 0 else tf
    for lo in range(0, tf, half):
        gate = _dot(xn, wg_ref[:, lo:lo + half])
        up = _dot(xn, wu_ref[:, lo:lo + half])
        act = (gate * jax.nn.sigmoid(gate) * up).astype(BF16)
        o_ref[...] += _dot(act, wo_ref[lo:lo + half, :])

    @pl.when(j == nf - 1)
    def _():
        o_ref[...] = x_ref[...] + _rms(o_ref[...], g_out_ref[...])


def _ffn(x, g_in, g_out, w_in, w_out, layer, tm, tf):
    m, d = x.shape
    f = w_out.shape[1]
    nf = f // tf
    return pl.pallas_call(
        functools.partial(_ffn_kernel, nf=nf),
        grid=(m // tm, nf),
        in_specs=[
            pl.BlockSpec((tm, d), lambda i, j: (i, 0)),
            pl.BlockSpec((1, d), lambda i, j: (0, 0)),
            pl.BlockSpec((1, d), lambda i, j: (0, 0)),
            pl.BlockSpec((None, d, tf), lambda i, j: (layer, 0, j)),
            pl.BlockSpec((None, d, tf), lambda i, j: (layer, 0, j + nf)),
            pl.BlockSpec((None, tf, d), lambda i, j: (layer, j, 0)),
        ],
        out_specs=pl.BlockSpec((tm, d), lambda i, j: (i, 0)),
        out_shape=jax.ShapeDtypeStruct((m, d), F32),
        scratch_shapes=[pltpu.VMEM((tm, d), BF16)],
        compiler_params=_cparams("parallel", "arbitrary"),
        name="ffn",
    )(x, g_in, g_out, w_in, w_in, w_out)


def _flash_kernel(q_ref, k_ref, v_ref, o_ref, m_ref, l_ref, acc_ref, *, tq):
    qi = pl.program_id(2)
    heads = FLASH_HEADS
    qts = [q_ref[:, h * HEAD_PAD:(h + 1) * HEAD_PAD].astype(F32).T.astype(BF16) for h in range(heads)]
    m_ref[...] = jnp.full(m_ref.shape, -jnp.inf, F32)
    l_ref[...] = jnp.zeros(l_ref.shape, F32)
    acc_ref[...] = jnp.zeros(acc_ref.shape, F32)

    def step(kc, diagonal):
        start = pl.multiple_of(kc * tq, tq)
        scores = [_dot(k_ref[pl.ds(start, tq), h * HEAD_PAD:(h + 1) * HEAD_PAD], qts[h]) for h in range(heads)]
        for h in range(heads):
            v = v_ref[pl.ds(start, tq), h * V_DIM:(h + 1) * V_DIM]
            s = scores[h]
            if diagonal:
                key = lax.broadcasted_iota(jnp.int32, s.shape, 0)
                query = lax.broadcasted_iota(jnp.int32, s.shape, 1)
                s = jnp.where(key <= query, s, NEG_INF)
            m_prev = m_ref[h]
            m_new = jnp.maximum(m_prev, jnp.max(s, axis=0, keepdims=True))
            alpha = jnp.exp2(m_prev - m_new)
            p = jnp.exp2(s - m_new)
            l_ref[h] = alpha * l_ref[h] + jnp.sum(p, axis=0, keepdims=True)
            acc_ref[h] = alpha * acc_ref[h] + _dot_tn(v, p.astype(BF16))
            m_ref[h] = m_new

    def body(kc, carry):
        step(kc, False)
        return carry

    lax.fori_loop(0, qi, body, 0)
    step(qi, True)
    for h in range(heads):
        o_ref[:, h * V_DIM:(h + 1) * V_DIM] = (acc_ref[h] / l_ref[h]).T.astype(o_ref.dtype)


def _mla_prompt_attention(q, kv, batch, seq, tq):
    nq = seq // tq
    hb = FLASH_HEADS
    v_block0 = MLA_HEADS * HEAD_PAD // (hb * V_DIM)
    return pl.pallas_call(
        functools.partial(_flash_kernel, tq=tq),
        grid=(batch, MLA_HEADS // hb, nq),
        in_specs=[
            pl.BlockSpec((tq, hb * HEAD_PAD), lambda b, h, i: (b * nq + i, h)),
            pl.BlockSpec((seq, hb * HEAD_PAD), lambda b, h, i: (b, h)),
            pl.BlockSpec((seq, hb * V_DIM), lambda b, h, i: (b, v_block0 + h)),
        ],
        out_specs=pl.BlockSpec((tq, hb * V_DIM), lambda b, h, i: (b * nq + i, h)),
        out_shape=jax.ShapeDtypeStruct((batch * seq, MLA_HEADS * V_DIM), BF16),
        scratch_shapes=[
            pltpu.VMEM((hb, 1, tq), F32),
            pltpu.VMEM((hb, 1, tq), F32),
            pltpu.VMEM((hb, V_DIM, tq), F32),
        ],
        compiler_params=_cparams("parallel", "parallel", "arbitrary"),
        name="mla_prompt_attention",
    )(q, kv, kv)


def _head_mm_kernel(a_ref, w_ref, o_ref):
    o_ref[...] = _dot(a_ref[...], w_ref[...]).astype(o_ref.dtype)


def _q_absorb(q_s, w_uk_t):
    db = q_s.shape[0]
    return pl.pallas_call(
        _head_mm_kernel,
        grid=(MLA_HEADS,),
        in_specs=[
            pl.BlockSpec((db, NOPE_DIM), lambda h: (0, 2 * h)),
            pl.BlockSpec((None, NOPE_DIM, KV_LORA), lambda h: (h, 0, 0)),
        ],
        out_specs=pl.BlockSpec((None, db, KV_LORA), lambda h: (h, 0, 0)),
        out_shape=jax.ShapeDtypeStruct((MLA_HEADS, db, KV_LORA), BF16),
        compiler_params=_cparams("parallel"),
        name="mla_q_absorb",
    )(q_s, w_uk_t)


def _v_up(o_lat_t, w_uv_t):
    db = o_lat_t.shape[1]
    return pl.pallas_call(
        _head_mm_kernel,
        grid=(MLA_HEADS,),
        in_specs=[
            pl.BlockSpec((None, db, KV_LORA), lambda h: (h, 0, 0)),
            pl.BlockSpec((None, KV_LORA, V_DIM), lambda h: (h, 0, 0)),
        ],
        out_specs=pl.BlockSpec((db, V_DIM), lambda h: (0, h)),
        out_shape=jax.ShapeDtypeStruct((db, MLA_HEADS * V_DIM), BF16),
        compiler_params=_cparams("parallel"),
        name="mla_v_up",
    )(o_lat_t, w_uv_t)


def _mla_decode_kernel(pt_ref, q_ref, new_ref, cache_ref, o_ref, pages_ref, kvt0_ref, kvt1_ref, s0_ref, s1_ref,
                       m_ref, l_ref, acc_ref, sem, *, n_pages, n_chunks, n_steps):
    kvt_refs = (kvt0_ref, kvt1_ref)
    s_refs = (s0_ref, s1_ref)
    t = pl.program_id(0)
    prev_chunk = (t + n_chunks - 1) % n_chunks

    def page_copy(step, p):
        chunk = jnp.minimum(step, n_steps - 2)
        slot = step % DEC_SLOTS
        return pltpu.make_async_copy(cache_ref.at[pt_ref[chunk * n_pages + p]], pages_ref.at[slot, p], sem.at[slot])

    def start_chunk(step):
        for p in range(n_pages):
            page_copy(step, p).start(priority=p % 2)

    def wait_chunk(step):
        slot = step % DEC_SLOTS
        pltpu.make_async_copy(cache_ref.at[pl.ds(0, n_pages)], pages_ref.at[slot], sem.at[slot]).wait()

    @pl.when(t == 0)
    def _():
        for step in range(DEC_SLOTS - 1):
            start_chunk(step)
        kvt_refs[1][...] = jnp.zeros(kvt_refs[1].shape, BF16)
        s_refs[1][...] = jnp.zeros(s_refs[1].shape, F32)
        m_ref[...] = jnp.zeros(m_ref.shape, F32)
        l_ref[...] = jnp.zeros(l_ref.shape, F32)
        acc_ref[...] = jnp.zeros(acc_ref.shape, F32)

    @pl.when(jnp.logical_and(t > 0, prev_chunk == 0))
    def _():
        new = new_ref[...].astype(BF16).astype(F32)
        m_ref[...] = jnp.sum(q_ref[...].astype(F32) * new, axis=-1, keepdims=True) * MLA_SCALE
        l_ref[...] = jnp.ones(l_ref.shape, F32)
        acc_ref[...] = jnp.broadcast_to(new, acc_ref.shape)

    def work(cur, prev):
        kvt_cur, kvt_prev = kvt_refs[cur], kvt_refs[prev]
        slot = t % DEC_SLOTS
        start_chunk(t + DEC_SLOTS - 1)
        wait_chunk(t)
        for p in range(n_pages):
            kvt_cur[:, p * PAGE_SIZE:(p + 1) * PAGE_SIZE] = pages_ref[slot, p].astype(BF16)
        s_refs[cur][...] = _dot(q_ref[...], kvt_cur[...]) * MLA_SCALE
        s = s_refs[prev][...]
        m_prev = m_ref[...]
        m_new = jnp.maximum(m_prev, jnp.max(s, axis=-1, keepdims=True))
        alpha = jnp.exp(m_prev - m_new)
        p = jnp.exp(s - m_new)
        l_ref[...] = alpha * l_ref[...] + jnp.sum(p, axis=-1, keepdims=True)
        acc_ref[...] = alpha * acc_ref[...] + _dot_nt(p.astype(BF16), kvt_prev[...])
        m_ref[...] = m_new

    @pl.when(t % 2 == 0)
    def _():
        work(0, 1)

    @pl.when(t % 2 == 1)
    def _():
        work(1, 0)

    @pl.when(jnp.logical_and(t > 0, prev_chunk == n_chunks - 1))
    def _():
        o_ref[...] = (acc_ref[...] / l_ref[...])[:, :KV_LORA]

    @pl.when(t == n_steps - 1)
    def _():
        for step in range(n_steps, n_steps + DEC_SLOTS - 1):
            wait_chunk(step)


def _mla_decode(q_cat, rows_new, cache_t, page_table):
    db = q_cat.shape[0]
    n_tab = page_table.shape[1]
    n_pages = min(DEC_PAGES, n_tab // 2)
    n_chunks = n_tab // n_pages
    assert n_chunks >= 2 and n_chunks * n_pages == n_tab
    n_steps = db * n_chunks + 1
    width = n_pages * PAGE_SIZE
    chunk = lambda t: jnp.minimum(t, n_steps - 2)
    batch_of = lambda t: chunk(t) // n_chunks

    grid_spec = pltpu.PrefetchScalarGridSpec(
        num_scalar_prefetch=1,
        grid=(n_steps,),
        in_specs=[
            pl.BlockSpec((None, MLA_HEADS, ROW_DIM), lambda t, pt: (batch_of(t), 0, 0)),
            pl.BlockSpec((None, 1, ROW_DIM), lambda t, pt: (batch_of(t), 0, 0)),
            pl.BlockSpec(memory_space=pl.ANY),
        ],
        out_specs=pl.BlockSpec((None, MLA_HEADS, KV_LORA),
                               lambda t, pt: (jnp.maximum(t - 1, 0) // n_chunks, 0, 0)),
        scratch_shapes=[
            pltpu.VMEM((DEC_SLOTS, n_pages, ROW_DIM, PAGE_SIZE), F32),
            pltpu.VMEM((ROW_DIM, width), BF16),
            pltpu.VMEM((ROW_DIM, width), BF16),
            pltpu.VMEM((MLA_HEADS, width), F32),
            pltpu.VMEM((MLA_HEADS, width), F32),
            pltpu.VMEM((MLA_HEADS, 1), F32),
            pltpu.VMEM((MLA_HEADS, 1), F32),
            pltpu.VMEM((MLA_HEADS, ROW_DIM), F32),
            pltpu.SemaphoreType.DMA((DEC_SLOTS,)),
        ],
    )
    return pl.pallas_call(
        functools.partial(_mla_decode_kernel, n_pages=n_pages, n_chunks=n_chunks, n_steps=n_steps),
        grid_spec=grid_spec,
        out_shape=jax.ShapeDtypeStruct((db, MLA_HEADS, KV_LORA), F32),
        compiler_params=_cparams("arbitrary"),
        name="mla_decode",
    )(page_table.reshape(-1), q_cat, rows_new.reshape(db, 1, ROW_DIM), cache_t)


def _dil_proj_kernel(x_ref, gq_ref, gkv_ref, wq_ref, wk_ref, wv_ref, qkv_ref, kv_ref, y_ref, *, dil):
    x = x_ref[...]
    xr = x * lax.rsqrt(jnp.mean(x * x, axis=-1, keepdims=True) + EPS)
    xq = (xr * gq_ref[...]).astype(BF16)
    xkv = (xr * gkv_ref[...]).astype(BF16)
    tm = x.shape[0]
    n = tm // dil
    tw = kv_ref.shape[0]
    for j, w_ref in enumerate((wq_ref, wk_ref, wv_ref)):
        y = _dot(xq if j == 0 else xkv, w_ref[...])
        if j > 0:
            kv_ref[:, (j - 1) * GROUP_DIM:j * GROUP_DIM] = y[tm - tw:, :]
        if dil == 1:
            qkv_ref[:, j * GROUP_DIM:(j + 1) * GROUP_DIM] = y.astype(BF16)
            continue
        for h in range(GROUP_HEADS):
            y_ref[h] = y[:, h * B_HEAD_DIM:(h + 1) * B_HEAD_DIM]
        for r in range(dil):
            for h in range(GROUP_HEADS):
                lo = r * QKV_DIM + j * GROUP_DIM + h * B_HEAD_DIM
                qkv_ref[:, lo:lo + B_HEAD_DIM] = y_ref[h, pl.ds(r, n, stride=dil), :].astype(BF16)


def _dil_proj(x, gq, gkv, w_q, w_kv, g, dil, window, seq, tm):
    m, d = x.shape
    tiles_per_seq = seq // tm
    tw = min(tm, window)
    win_tiles = window // tw
    assert window <= seq and window % tw == 0 and tm % tw == 0

    def win_block(i):
        return (i // tiles_per_seq) * win_tiles + jnp.maximum(i % tiles_per_seq - (tiles_per_seq - win_tiles), 0)

    w_block = (d, GROUP_DIM)
    return pl.pallas_call(
        functools.partial(_dil_proj_kernel, dil=dil),
        grid=(m // tm,),
        in_specs=[
            pl.BlockSpec((tm, d), lambda i: (i, 0)),
            pl.BlockSpec((1, d), lambda i: (0, 0)),
            pl.BlockSpec((1, d), lambda i: (0, 0)),
            pl.BlockSpec(w_block, lambda i: (0, g)),
            pl.BlockSpec(w_block, lambda i: (0, g)),
            pl.BlockSpec(w_block, lambda i: (0, N_GROUPS + g)),
        ],
        out_specs=[
            pl.BlockSpec((tm // dil, dil * QKV_DIM), lambda i: (i, 0)),
            pl.BlockSpec((tw, 2 * GROUP_DIM), lambda i: (win_block(i), 0)),
        ],
        out_shape=[
            jax.ShapeDtypeStruct((m // dil, dil * QKV_DIM), BF16),
            jax.ShapeDtypeStruct((m // seq * window, 2 * GROUP_DIM), F32),
        ],
        scratch_shapes=[pltpu.VMEM((GROUP_HEADS, tm, B_HEAD_DIM), F32)],
        compiler_params=_cparams("arbitrary"),
        name="dilated_proj",
    )(x, gq, gkv, w_q, w_kv, w_kv)


def _dil_prompt_kernel(q_ref, kc_ref, kp_ref, vc_ref, vp_ref, bias_ref, o_ref, lse_ref):
    first = pl.program_id(2) == 0
    heads = [slice(h * B_HEAD_DIM, (h + 1) * B_HEAD_DIM) for h in range(GROUP_HEADS)]
    scores = [_dot_nt(q_ref[:, hs], jnp.concatenate([kp_ref[:, hs], kc_ref[:, hs]], axis=0)) for hs in heads]
    for h, hs in enumerate(heads):
        v = jnp.concatenate([vp_ref[:, hs], vc_ref[:, hs]], axis=0)
        s = scores[h] * B_SCALE + bias_ref[h]
        col = lax.broadcasted_iota(jnp.int32, s.shape, 1)
        s = jnp.where(jnp.logical_and(first, col < DIL_TILE), NEG_INF, s)
        m = jnp.max(s, axis=-1, keepdims=True)
        p = jnp.exp(s - m)
        l = jnp.sum(p, axis=-1, keepdims=True)
        o_ref[:, hs] = _dot(p.astype(BF16), v) / l
        lse_ref[:, hs] = jnp.broadcast_to(m + jnp.log(l), (DIL_TILE, B_HEAD_DIM))


def _dil_prompt(qkv, bias, dil, batch, seq):
    sd = seq // dil
    nq = sd // DIL_TILE
    blk = (DIL_TILE, GROUP_DIM)
    cur = lambda b, i: b * nq + i
    prev = lambda b, i: b * nq + jnp.maximum(i - 1, 0)
    return pl.pallas_call(
        _dil_prompt_kernel,
        grid=(batch, dil, nq),
        in_specs=[
            pl.BlockSpec(blk, lambda b, r, i: (cur(b, i), 3 * r)),
            pl.BlockSpec(blk, lambda b, r, i: (cur(b, i), 3 * r + 1)),
            pl.BlockSpec(blk, lambda b, r, i: (prev(b, i), 3 * r + 1)),
            pl.BlockSpec(blk, lambda b, r, i: (cur(b, i), 3 * r + 2)),
            pl.BlockSpec(blk, lambda b, r, i: (prev(b, i), 3 * r + 2)),
            pl.BlockSpec((GROUP_HEADS, DIL_TILE, 2 * DIL_TILE), lambda b, r, i: (0, 0, 0)),
        ],
        out_specs=[
            pl.BlockSpec(blk, lambda b, r, i: (cur(b, i), r)),
            pl.BlockSpec(blk, lambda b, r, i: (cur(b, i), r)),
        ],
        out_shape=[
            jax.ShapeDtypeStruct((batch * sd, dil * GROUP_DIM), F32),
            jax.ShapeDtypeStruct((batch * sd, dil * GROUP_DIM), F32),
        ],
        compiler_params=_cparams("parallel", "parallel", "arbitrary"),
        name="dilated_prompt",
    )(qkv, qkv, qkv, qkv, qkv, bias)


def _dil_sample_kernel(q_ref, kn_ref, vn_ref, buf_ref, bias_ref, o_ref, lse_ref, *, dil):
    row = lax.broadcasted_iota(jnp.int32, (SUBLANES, B_HEAD_DIM), 0)
    row_wide = lax.broadcasted_iota(jnp.int32, (SUBLANES, GROUP_DIM), 0)
    own = row_wide == lax.broadcasted_iota(jnp.int32, (SUBLANES, GROUP_DIM), 1) // B_HEAD_DIM
    bias_keys = bias_ref[:, :DIL_KEYS]
    bias_new = bias_ref[:, DIL_KEYS:DIL_KEYS + 1]

    if dil == 1:
        rows_ref, stride = buf_ref, BUF_ROWS
    else:
        rows_ref, stride = buf_ref.reshape(DIL_SAMPLE_BLOCK, DIL_KEYS * BUF_ROWS_PAD, B_HEAD_DIM), BUF_ROWS_PAD

    def cached(b, r):
        return rows_ref[b, pl.ds(r, DIL_KEYS, stride=stride), :]

    for b in range(DIL_SAMPLE_BLOCK):
        qb = q_ref[b:b + 1, :]
        s = jnp.zeros((SUBLANES, DIL_KEYS), F32)
        for h in range(GROUP_HEADS):
            qh = jnp.broadcast_to(qb[:, h * B_HEAD_DIM:(h + 1) * B_HEAD_DIM], (SUBLANES, B_HEAD_DIM))
            sh = _dot_nt(qh.astype(BF16), cached(b, 2 * h).astype(BF16))
            s = jnp.where(row == h, sh, s)
        s = s * B_SCALE + bias_keys
        kn = kn_ref[b:b + 1, :].astype(BF16).astype(F32)
        vn = vn_ref[b:b + 1, :].astype(BF16).astype(F32)
        q_own = jnp.where(own, qb, 0.0).astype(BF16).astype(F32)
        s_new = jnp.sum(q_own * kn, axis=-1, keepdims=True) * B_SCALE + bias_new
        m = jnp.maximum(jnp.max(s, axis=-1, keepdims=True), s_new)
        p = jnp.exp(s - m)
        p_new = jnp.exp(s_new - m)
        l = jnp.sum(p, axis=-1, keepdims=True) + p_new
        p16 = p.astype(BF16)
        pn = p_new.astype(BF16).astype(F32)
        lse = m + jnp.log(l)
        for h in range(GROUP_HEADS):
            hs = slice(h * B_HEAD_DIM, (h + 1) * B_HEAD_DIM)
            oh = (_dot(p16, cached(b, 2 * h + 1).astype(BF16)) + pn * vn[:, hs]) / l
            o_ref[b:b + 1, hs] = jnp.sum(jnp.where(row == h, oh, 0.0), axis=0, keepdims=True)
            lse_ref[b:b + 1, hs] = jnp.sum(jnp.where(row == h, lse, 0.0), axis=0, keepdims=True)


def _dil_sample(q_s, kv_s, buf, bias, g, dil):
    db, length = buf.shape[:2]
    bb = DIL_SAMPLE_BLOCK
    rows = jnp.transpose(buf, (0, 1, 3, 2, 4))
    if dil == 1:
        rows = rows.reshape(db, length * BUF_ROWS, B_HEAD_DIM)
        buf_spec = pl.BlockSpec((bb, length * BUF_ROWS, B_HEAD_DIM), lambda i: (i, 0, 0))
    else:
        rows = rows.reshape(db, length // dil, dil * BUF_ROWS, B_HEAD_DIM)
        buf_spec = pl.BlockSpec((bb, DIL_KEYS, BUF_ROWS_PAD, B_HEAD_DIM), lambda i: (i, 0, 0, 0))
    return pl.pallas_call(
        functools.partial(_dil_sample_kernel, dil=dil),
        grid=(db // bb,),
        in_specs=[
            pl.BlockSpec((bb, GROUP_DIM), lambda i: (i, g)),
            pl.BlockSpec((bb, GROUP_DIM), lambda i: (i, g)),
            pl.BlockSpec((bb, GROUP_DIM), lambda i: (i, N_GROUPS + g)),
            buf_spec,
            pl.BlockSpec((SUBLANES, 2 * DIL_KEYS), lambda i: (0, 0)),
        ],
        out_specs=[
            pl.BlockSpec((bb, GROUP_DIM), lambda i: (i, 0)),
            pl.BlockSpec((bb, GROUP_DIM), lambda i: (i, 0)),
        ],
        out_shape=[
            jax.ShapeDtypeStruct((db, GROUP_DIM), F32),
            jax.ShapeDtypeStruct((db, GROUP_DIM), F32),
        ],
        compiler_params=_cparams("parallel"),
        name="dilated_sample",
    )(q_s, kv_s, kv_s, rows, bias)


def _merge_out_kernel(o0, o1, o2, l0, l1, l2, w_ref, x_ref, g_ref, out_ref, merged_ref, *scratch, dils):
    srcs = ((o0, l0), (o1, l1), (o2, l2))
    tm = x_ref.shape[0]
    ordered = []
    si = 0
    for (o_src, l_src), dil in zip(srcs, dils):
        if dil == 1:
            ordered.append(None)
            continue
        o_dst, l_dst = scratch[si], scratch[si + 1]
        si += 2
        n = tm // dil
        for r in range(dil):
            for h in range(GROUP_HEADS):
                lo = r * GROUP_DIM + h * B_HEAD_DIM
                o_dst[h, pl.ds(r, n, stride=dil), :] = o_src[:, lo:lo + B_HEAD_DIM]
                l_dst[h, pl.ds(r, n, stride=dil), :] = l_src[:, lo:lo + B_HEAD_DIM]
        ordered.append((o_dst, l_dst))
    for h in range(GROUP_HEADS):
        hs = slice(h * B_HEAD_DIM, (h + 1) * B_HEAD_DIM)
        os_, ls_ = [], []
        for (o_src, l_src), dst in zip(srcs, ordered):
            os_.append(o_src[:, hs] if dst is None else dst[0][h])
            ls_.append(l_src[:, hs] if dst is None else dst[1][h])
        mx = jnp.maximum(jnp.maximum(ls_[0], ls_[1]), ls_[2])
        e0, e1, e2 = jnp.exp(ls_[0] - mx), jnp.exp(ls_[1] - mx), jnp.exp(ls_[2] - mx)
        merged = (e0 * os_[0] + e1 * os_[1] + e2 * os_[2]) / (e0 + e1 + e2)
        merged_ref[:, hs] = merged.astype(BF16)
    y = _dot(merged_ref[...], w_ref[...])
    out_ref[...] = x_ref[...] + _rms(y, g_ref[...])


def _merge_out(outs, lses, dils, w, x, g, tm):
    m, d = x.shape
    parts = [pl.BlockSpec((tm // dil, dil * GROUP_DIM), lambda i: (i, 0)) for dil in dils]
    scratch = [pltpu.VMEM((tm, GROUP_DIM), BF16)]
    for dil in dils:
        if dil > 1:
            scratch += [pltpu.VMEM((GROUP_HEADS, tm, B_HEAD_DIM), F32)] * 2
    return pl.pallas_call(
        functools.partial(_merge_out_kernel, dils=tuple(dils)),
        grid=(m // tm,),
        in_specs=parts + parts + [
            pl.BlockSpec((GROUP_DIM, d), lambda i: (0, 0)),
            pl.BlockSpec((tm, d), lambda i: (i, 0)),
            pl.BlockSpec((1, d), lambda i: (0, 0)),
        ],
        out_specs=pl.BlockSpec((tm, d), lambda i: (i, 0)),
        out_shape=jax.ShapeDtypeStruct((m, d), F32),
        scratch_shapes=scratch,
        compiler_params=_cparams("parallel"),
        name="dilated_merge_out",
    )(*outs, *lses, w, x, g)


def _t5_bucket(dist):
    max_exact = T5_BUCKETS // 2
    d = np.maximum(dist, 1).astype(np.float32)
    large = max_exact + (np.log(d / max_exact) / math.log(T5_MAX_DIST / max_exact)
                         * (T5_BUCKETS - max_exact)).astype(np.int32)
    large = np.minimum(large, T5_BUCKETS - 1)
    return np.where(dist < max_exact, dist, large).astype(np.int32)


def _rows_static(table, idx):
    pieces, start = [], 0
    for i in range(1, len(idx) + 1):
        if i == len(idx) or idx[i] != idx[start]:
            row = table[int(idx[start]):int(idx[start]) + 1]
            pieces.append(jnp.broadcast_to(row, (i - start,) + table.shape[1:]))
            start = i
    return jnp.concatenate(pieces, axis=0)


def _group_bias(t5_bias, g):
    window, dil = DIL_GROUPS[g]
    buckets = _t5_bucket(dil * np.arange(window // dil + 1))
    return _rows_static(t5_bias[:, g * GROUP_HEADS:(g + 1) * GROUP_HEADS].astype(F32), buckets)


def _prompt_bias(bias):
    n = 2 * DIL_TILE
    heads = bias.shape[1]
    period = jnp.concatenate([bias[::-1], jnp.full((n - DIL_KEYS, heads), NEG_INF, F32)], axis=0)
    band = jnp.tile(period, (DIL_TILE, 1))[:DIL_TILE * n].reshape(DIL_TILE, n, heads)
    return jnp.transpose(band, (2, 0, 1))


def _sample_bias(bias):
    keys = bias[1:][::-1].T
    own = jnp.broadcast_to(bias[0][:, None], (GROUP_HEADS, DIL_KEYS))
    both = jnp.concatenate([keys, own], axis=1)
    return jnp.pad(both, ((0, SUBLANES - GROUP_HEADS), (0, 0)))


def _rope_tables(pos):
    half = ROPE_DIM // 2
    inv = ROPE_THETA ** (-jnp.arange(half, dtype=F32) / half)
    ang = pos.astype(F32)[:, None] * inv
    cos, sin = jnp.cos(ang), jnp.sin(ang)
    zero = jnp.zeros((pos.shape[0], LANES - ROPE_DIM), F32)
    return (jnp.concatenate([cos, cos, zero], axis=1), jnp.concatenate([-sin, sin, zero], axis=1))


def _tile(n, pref):
    return pref if n % pref == 0 else n


def kernel(x_prompt, x_sample, cache_mla, page_table, cache_b_win0, cache_b_win1, cache_b_win2,
           norm_gains, w_mla_in, g_q_a, g_kv_a, w_uq, w_uk, w_uv, w_mla_o,
           g_kv_b, w_kv_b, w_q_b, w_b_o, t5_bias, w_ffn_in, w_ffn_out):
    batch, seq, d = x_prompt.shape
    db = x_sample.shape[0]
    assert x_sample.shape[1] == 1 and DEPTH == 2 and N_A_LAYERS == 1
    win_bufs = (cache_b_win0, cache_b_win1, cache_b_win2)
    for gi, (window, dil) in enumerate(DIL_GROUPS):
        assert win_bufs[gi].shape[1] == window and window // dil == DIL_KEYS
        assert seq % (dil * DIL_TILE) == 0
    mp = batch * seq
    xp = x_prompt.reshape(mp, d)
    xs = x_sample.reshape(db, d)
    gains = norm_gains.reshape(DEPTH, 4, 1, d)
    tm_p = _tile(mp, 1024)
    tm_h = _tile(mp, 512)

    rope_cols = jnp.pad(w_mla_in[0][:, Q_LORA + KV_LORA:], ((0, 0), (0, LANES - ROPE_DIM)))
    w_in_pad = jnp.concatenate([w_mla_in[0][:, :Q_LORA + KV_LORA], rope_cols], axis=1).astype(BF16)
    w_uq_pad = jnp.pad(w_uq[0].reshape(Q_LORA, MLA_HEADS, NOPE_DIM + ROPE_DIM),
                       ((0, 0), (0, 0), (0, HEAD_PAD - NOPE_DIM - ROPE_DIM)))
    w_uq_pad = w_uq_pad.reshape(Q_LORA, MLA_HEADS * HEAD_PAD).astype(BF16)
    w_k_lat = jnp.pad(w_uk[0], ((0, 0), (0, 0), (0, HEAD_PAD - NOPE_DIM)))
    rope_copy = jnp.pad(jnp.eye(ROPE_DIM, dtype=F32), ((0, 0), (NOPE_DIM, HEAD_PAD - NOPE_DIM - ROPE_DIM)))
    rope_copy = jnp.broadcast_to(rope_copy[:, None, :], (ROPE_DIM, MLA_HEADS, HEAD_PAD))
    w_k = jnp.concatenate([w_k_lat, rope_copy], axis=0).reshape(ROW_DIM, MLA_HEADS * HEAD_PAD)
    w_v = jnp.pad(w_uv[0].reshape(KV_LORA, MLA_HEADS * V_DIM), ((0, ROPE_DIM), (0, 0)))
    w_kv = jnp.concatenate([w_k, w_v], axis=1).astype(BF16)
    w_uk_t = jnp.transpose(w_uk[0], (1, 2, 0)).astype(BF16)
    w_uv_t = jnp.transpose(w_uv[0], (1, 0, 2)).astype(BF16)
    w_o = w_mla_o[0].astype(BF16)
    w_kvb = w_kv_b.astype(BF16)
    w_qb = w_q_b[0].astype(BF16)
    w_bo = w_b_o[0].astype(BF16)
    w_f_in = w_ffn_in.astype(BF16)
    w_f_out = w_ffn_out.astype(BF16)
    f = w_ffn_out.shape[1]
    tf = _tile(f, 512)

    c_p, s_p = _rope_tables(jnp.tile(jnp.arange(seq), batch))
    c_s, s_s = _rope_tables(jnp.full((db,), PAST_LEN, jnp.int32))

    g = gains[0]
    cq_p, rows_p = _mla_in(xp, g[0], w_in_pad, g_q_a[0:1], g_kv_a[0:1], c_p, s_p, tm_h)
    cq_s, rows_s = _mla_in(xs, g[0], w_in_pad, g_q_a[0:1], g_kv_a[0:1], c_s, s_s, db)
    q_p = _q_up(cq_p, w_uq_pad, c_p, s_p, tm_p, 1024, out_scale=MLA_SCALE * math.log2(math.e))
    q_s = _q_up(cq_s, w_uq_pad, c_s, s_s, db, 1024)

    kv_p = _cast_mm(rows_p, w_kv, BF16, tm_p, 2048)
    attn_p = _mla_prompt_attention(q_p, kv_p, batch, seq, _tile(seq, 512))
    xp = _mm_norm_res(attn_p, w_o, xp, g[1], tm_h, w_o.shape[0])

    q_lat = _q_absorb(q_s, w_uk_t)
    q_rope_s = q_s.reshape(db, MLA_HEADS, HEAD_PAD)[:, :, NOPE_DIM:NOPE_DIM + ROPE_DIM]
    q_cat = jnp.concatenate([jnp.transpose(q_lat, (1, 0, 2)), q_rope_s], axis=-1)
    cache_t = jnp.swapaxes(cache_mla[0], 1, 2)
    o_lat = _mla_decode(q_cat, rows_s, cache_t, page_table)
    attn_s = _v_up(jnp.transpose(o_lat, (1, 0, 2)).astype(BF16), w_uv_t)
    xs = _mm_norm_res(attn_s, w_o, xs, g[1], db, 512)

    xp = _ffn(xp, g[2], g[3], w_f_in, w_f_out, 0, tm_h, tf)
    xs = _ffn(xs, g[2], g[3], w_f_in, w_f_out, 0, db, tf)

    g = gains[1]
    g_kvb = g_kv_b.reshape(1, d)
    kvb_s = _norm_mm(xs, g_kvb, w_kvb, F32, db, 512)
    qb_s = _norm_mm(xs, g[0], w_qb, F32, db, GROUP_DIM)
    outs_p, lses_p, outs_s, lses_s, kv_groups = [], [], [], [], []
    dils = [dil for _, dil in DIL_GROUPS]
    biases = [_group_bias(t5_bias, gi) for gi in range(N_GROUPS)]
    band_all = _prompt_bias(jnp.concatenate(biases, axis=1))
    for gi, (window, dil) in enumerate(DIL_GROUPS):
        qkv, kv_g = _dil_proj(xp, g[0], g_kvb, w_qb, w_kvb, gi, dil, window, seq, tm_h)
        kv_groups.append(kv_g)
        bias = biases[gi]
        o, lse = _dil_prompt(qkv, band_all[gi * GROUP_HEADS:(gi + 1) * GROUP_HEADS], dil, batch, seq)
        outs_p.append(o)
        lses_p.append(lse)
        o, lse = _dil_sample(qb_s, kvb_s, win_bufs[gi], _sample_bias(bias), gi, dil)
        outs_s.append(o)
        lses_s.append(lse)
    xp = _merge_out(outs_p, lses_p, dils, w_bo, xp, g[1], _tile(mp, 256))
    xs = _merge_out(outs_s, lses_s, [1] * N_GROUPS, w_bo, xs, g[1], db)

    xp = _ffn(xp, g[2], g[3], w_f_in, w_f_out, 1, tm_h, tf)
    xs = _ffn(xs, g[2], g[3], w_f_in, w_f_out, 1, db, tf)

    kv6_s = kvb_s.reshape(db, 1, 2, N_GROUPS, GROUP_HEADS, B_HEAD_DIM)
    wins = []
    for gi, (window, _) in enumerate(DIL_GROUPS):
        wins.append(kv_groups[gi].reshape(batch, window, 2, GROUP_HEADS, B_HEAD_DIM))
        wins.append(kv6_s[:, :, :, gi])
    return (xp.reshape(batch, seq, d), xs.reshape(db, 1, d),
            rows_p.reshape(1, batch, seq, ROW_DIM), rows_s.reshape(1, db, 1, ROW_DIM), *wins)
```

```python
import functools
import math

import numpy as np
import jax
import jax.numpy as jnp
from jax import lax
from jax.experimental import pallas as pl
from jax.experimental.pallas import tpu as pltpu

D_MODEL = 2048
DEPTH = 2
PAST_LEN = 16384
PAGE_SIZE = 128
N_A_LAYERS = DEPTH // 2
MLA_HEADS = 16
Q_LORA = 512
KV_LORA = 512
NOPE_DIM = 128
ROPE_DIM = 64
V_DIM = 128
ROPE_THETA = 10000.0
MLA_SCALE = (NOPE_DIM + ROPE_DIM) ** -0.5
DIL_GROUPS = ((128, 1), (512, 4), (2048, 16))
N_GROUPS = len(DIL_GROUPS)
GROUP_HEADS = 6
B_HEAD_DIM = 128
B_HEADS = N_GROUPS * GROUP_HEADS
B_SCALE = B_HEAD_DIM ** -0.5
T5_BUCKETS = 32
T5_MAX_DIST = 2048
EPS = 1e-6
NEG_INF = -1e30

F32 = jnp.float32
BF16 = jnp.bfloat16

VMEM_LIMIT_BYTES = 56 * 1024 * 1024
LANES = 128
SUBLANES = 8
HEAD_PAD = 2 * LANES
ROW_DIM = KV_LORA + ROPE_DIM
GROUP_DIM = GROUP_HEADS * B_HEAD_DIM
QKV_DIM = 3 * GROUP_DIM
DIL_KEYS = 128
DIL_TILE = 128
FLASH_HEADS = 4
DEC_PAGES = 32
DEC_SLOTS = 3
DIL_SAMPLE_BLOCK = 8
BUF_ROWS = 2 * GROUP_HEADS
BUF_ROWS_PAD = 16


def _cparams(*semantics):
    return pltpu.CompilerParams(dimension_semantics=semantics, vmem_limit_bytes=VMEM_LIMIT_BYTES)


def _rms(x, g):
    return x * lax.rsqrt(jnp.mean(x * x, axis=-1, keepdims=True) + EPS) * g


def _dot(a, b):
    return jnp.dot(a, b, preferred_element_type=F32)


def _dot_nt(a, b):
    return lax.dot_general(a, b, (((1,), (1,)), ((), ())), preferred_element_type=F32)


def _dot_tn(a, b):
    return lax.dot_general(a, b, (((0,), (0,)), ((), ())), preferred_element_type=F32)


def _rope128(x, c, s):
    lane = lax.broadcasted_iota(jnp.int32, x.shape, 1)
    half = ROPE_DIM // 2
    swapped = jnp.where(lane < half, pltpu.roll(x, LANES - half, 1), pltpu.roll(x, half, 1))
    return x * c + swapped * s


def _mla_in_kernel(x_ref, g_ref, w_ref, gq_ref, gkv_ref, c_ref, s_ref, cq_ref, rows_ref):
    tm = x_ref.shape[0]
    half = tm // 2 if tm % (2 * BUF_ROWS_PAD) == 0 else tm
    for lo in range(0, tm, half):
        rs = slice(lo, lo + half)
        xn = _rms(x_ref[rs, :], g_ref[...]).astype(BF16)
        a = _dot(xn, w_ref[...])
        cq_ref[rs, :] = _rms(a[:, :Q_LORA], gq_ref[...]).astype(BF16)
        rows_ref[rs, :KV_LORA] = _rms(a[:, Q_LORA:Q_LORA + KV_LORA], gkv_ref[...])
        kr = _rope128(a[:, Q_LORA + KV_LORA:], c_ref[rs, :], s_ref[rs, :])
        rows_ref[rs, KV_LORA:] = kr[:, :ROPE_DIM]


def _mla_in(x, g, w_pad, gq, gkv, c_tab, s_tab, tm):
    m, d = x.shape
    n = w_pad.shape[1]
    return pl.pallas_call(
        _mla_in_kernel,
        grid=(m // tm,),
        in_specs=[
            pl.BlockSpec((tm, d), lambda i: (i, 0)),
            pl.BlockSpec((1, d), lambda i: (0, 0)),
            pl.BlockSpec((d, n), lambda i: (0, 0)),
            pl.BlockSpec((1, Q_LORA), lambda i: (0, 0)),
            pl.BlockSpec((1, KV_LORA), lambda i: (0, 0)),
            pl.BlockSpec((tm, LANES), lambda i: (i, 0)),
            pl.BlockSpec((tm, LANES), lambda i: (i, 0)),
        ],
        out_specs=[
            pl.BlockSpec((tm, Q_LORA), lambda i: (i, 0)),
            pl.BlockSpec((tm, ROW_DIM), lambda i: (i, 0)),
        ],
        out_shape=[
            jax.ShapeDtypeStruct((m, Q_LORA), BF16),
            jax.ShapeDtypeStruct((m, ROW_DIM), F32),
        ],
        compiler_params=_cparams("parallel"),
        name="mla_in",
    )(x, g, w_pad, gq, gkv, c_tab, s_tab)


def _q_up_kernel(cq_ref, w_ref, c_ref, s_ref, o_ref, *, heads_per_tile, out_scale):
    a = _dot(cq_ref[...], w_ref[...])
    if out_scale != 1.0:
        a = a * out_scale
    c = c_ref[...]
    s = s_ref[...]
    for h in range(heads_per_tile):
        lo = h * HEAD_PAD
        o_ref[:, lo:lo + LANES] = a[:, lo:lo + LANES].astype(BF16)
        o_ref[:, lo + LANES:lo + HEAD_PAD] = _rope128(a[:, lo + LANES:lo + HEAD_PAD], c, s).astype(BF16)


def _q_up(cq, w_pad, c_tab, s_tab, tm, tn, out_scale=1.0):
    m, k = cq.shape
    n = w_pad.shape[1]
    return pl.pallas_call(
        functools.partial(_q_up_kernel, heads_per_tile=tn // HEAD_PAD, out_scale=out_scale),
        grid=(m // tm, n // tn),
        in_specs=[
            pl.BlockSpec((tm, k), lambda i, j: (i, 0)),
            pl.BlockSpec((k, tn), lambda i, j: (0, j)),
            pl.BlockSpec((tm, LANES), lambda i, j: (i, 0)),
            pl.BlockSpec((tm, LANES), lambda i, j: (i, 0)),
        ],
        out_specs=pl.BlockSpec((tm, tn), lambda i, j: (i, j)),
        out_shape=jax.ShapeDtypeStruct((m, n), BF16),
        compiler_params=_cparams("parallel", "parallel"),
        name="mla_q_up",
    )(cq, w_pad, c_tab, s_tab)


def _cast_mm_kernel(x_ref, w_ref, o_ref):
    o_ref[...] = _dot(x_ref[...].astype(BF16), w_ref[...]).astype(o_ref.dtype)


def _cast_mm(x, w, out_dtype, tm, tn):
    m, k = x.shape
    n = w.shape[1]
    return pl.pallas_call(
        _cast_mm_kernel,
        grid=(m // tm, n // tn),
        in_specs=[
            pl.BlockSpec((tm, k), lambda i, j: (i, 0)),
            pl.BlockSpec((k, tn), lambda i, j: (0, j)),
        ],
        out_specs=pl.BlockSpec((tm, tn), lambda i, j: (i, j)),
        out_shape=jax.ShapeDtypeStruct((m, n), out_dtype),
        compiler_params=_cparams("parallel", "parallel"),
        name="cast_mm",
    )(x, w)


def _norm_mm_kernel(x_ref, g_ref, w_ref, o_ref, xn_ref):
    @pl.when(pl.program_id(1) == 0)
    def _():
        xn_ref[...] = _rms(x_ref[...], g_ref[...]).astype(BF16)

    o_ref[...] = _dot(xn_ref[...], w_ref[...]).astype(o_ref.dtype)


def _norm_mm(x, g, w, out_dtype, tm, tn):
    m, d = x.shape
    n = w.shape[1]
    return pl.pallas_call(
        _norm_mm_kernel,
        grid=(m // tm, n // tn),
        in_specs=[
            pl.BlockSpec((tm, d), lambda i, j: (i, 0)),
            pl.BlockSpec((1, d), lambda i, j: (0, 0)),
            pl.BlockSpec((d, tn), lambda i, j: (0, j)),
        ],
        out_specs=pl.BlockSpec((tm, tn), lambda i, j: (i, j)),
        out_shape=jax.ShapeDtypeStruct((m, n), out_dtype),
        scratch_shapes=[pltpu.VMEM((tm, d), BF16)],
        compiler_params=_cparams("parallel", "arbitrary"),
        name="norm_mm",
    )(x, g, w)


def _mm_norm_res_kernel(a_ref, w_ref, x_ref, g_ref, o_ref, *, nk):
    if nk == 1:
        o_ref[...] = x_ref[...] + _rms(_dot(a_ref[...], w_ref[...]), g_ref[...])
        return
    k = pl.program_id(1)

    @pl.when(k == 0)
    def _():
        o_ref[...] = jnp.zeros(o_ref.shape, F32)

    o_ref[...] += _dot(a_ref[...], w_ref[...])

    @pl.when(k == nk - 1)
    def _():
        o_ref[...] = x_ref[...] + _rms(o_ref[...], g_ref[...])


def _mm_norm_res(a, w, x, g, tm, tk):
    m, kdim = a.shape
    d = w.shape[1]
    nk = kdim // tk
    return pl.pallas_call(
        functools.partial(_mm_norm_res_kernel, nk=nk),
        grid=(m // tm, nk),
        in_specs=[
            pl.BlockSpec((tm, tk), lambda i, k: (i, k)),
            pl.BlockSpec((tk, d), lambda i, k: (k, 0)),
            pl.BlockSpec((tm, d), lambda i, k: (i, 0)),
            pl.BlockSpec((1, d), lambda i, k: (0, 0)),
        ],
        out_specs=pl.BlockSpec((tm, d), lambda i, k: (i, 0)),
        out_shape=jax.ShapeDtypeStruct((m, d), F32),
        compiler_params=_cparams("parallel", "arbitrary"),
        name="mm_norm_res",
    )(a, w, x, g)


def _ffn_kernel(x_ref, g_in_ref, g_out_ref, wg_ref, wu_ref, wo_ref, o_ref, xn_ref, *, nf):
    j = pl.program_id(1)

    @pl.when(j == 0)
    def _():
        xn_ref[...] = _rms(x_ref[...], g_in_ref[...]).astype(BF16)
        o_ref[...] = jnp.zeros(o_ref.shape, F32)

    xn = xn_ref[...]
    tf = wo_ref.shape[0]
    half = tf // 2 if tf % (2 * LANES) == 0 else tf
    for lo in range(0, tf, half):
        gate = _dot(xn, wg_ref[:, lo:lo + half])
        up = _dot(xn, wu_ref[:, lo:lo + half])
        act = (gate * jax.nn.sigmoid(gate) * up).astype(BF16)
        o_ref[...] += _dot(act, wo_ref[lo:lo + half, :])

    @pl.when(j == nf - 1)
    def _():
        o_ref[...] = x_ref[...] + _rms(o_ref[...], g_out_ref[...])


def _ffn(x, g_in, g_out, w_in, w_out, layer, tm, tf):
    m, d = x.shape
    f = w_out.shape[1]
    nf = f // tf
    return pl.pallas_call(
        functools.partial(_ffn_kernel, nf=nf),
        grid=(m // tm, nf),
        in_specs=[
            pl.BlockSpec((tm, d), lambda i, j: (i, 0)),
            pl.BlockSpec((1, d), lambda i, j: (0, 0)),
            pl.BlockSpec((1, d), lambda i, j: (0, 0)),
            pl.BlockSpec((None, d, tf), lambda i, j: (layer, 0, j)),
            pl.BlockSpec((None, d, tf), lambda i, j: (layer, 0, j + nf)),
            pl.BlockSpec((None, tf, d), lambda i, j: (layer, j, 0)),
        ],
        out_specs=pl.BlockSpec((tm, d), lambda i, j: (i, 0)),
        out_shape=jax.ShapeDtypeStruct((m, d), F32),
        scratch_shapes=[pltpu.VMEM((tm, d), BF16)],
        compiler_params=_cparams("parallel", "arbitrary"),
        name="ffn",
    )(x, g_in, g_out, w_in, w_in, w_out)


def _flash_kernel(q_ref, k_ref, v_ref, o_ref, m_ref, l_ref, acc_ref, *, tq):
    qi = pl.program_id(2)
    heads = FLASH_HEADS
    qts = [q_ref[:, h * HEAD_PAD:(h + 1) * HEAD_PAD].astype(F32).T.astype(BF16) for h in range(heads)]
    m_ref[...] = jnp.full(m_ref.shape, -jnp.inf, F32)
    l_ref[...] = jnp.zeros(l_ref.shape, F32)
    acc_ref[...] = jnp.zeros(acc_ref.shape, F32)

    def step(kc, diagonal):
        start = pl.multiple_of(kc * tq, tq)
        scores = [_dot(k_ref[pl.ds(start, tq), h * HEAD_PAD:(h + 1) * HEAD_PAD], qts[h]) for h in range(heads)]
        for h in range(heads):
            v = v_ref[pl.ds(start, tq), h * V_DIM:(h + 1) * V_DIM]
            s = scores[h]
            if diagonal:
                key = lax.broadcasted_iota(jnp.int32, s.shape, 0)
                query = lax.broadcasted_iota(jnp.int32, s.shape, 1)
                s = jnp.where(key <= query, s, NEG_INF)
            m_prev = m_ref[h]
            m_new = jnp.maximum(m_prev, jnp.max(s, axis=0, keepdims=True))
            alpha = jnp.exp2(m_prev - m_new)
            p = jnp.exp2(s - m_new)
            l_ref[h] = alpha * l_ref[h] + jnp.sum(p, axis=0, keepdims=True)
            acc_ref[h] = alpha * acc_ref[h] + _dot_tn(v, p.astype(BF16))
            m_ref[h] = m_new

    def body(kc, carry):
        step(kc, False)
        return carry

    lax.fori_loop(0, qi, body, 0)
    step(qi, True)
    for h in range(heads):
        o_ref[:, h * V_DIM:(h + 1) * V_DIM] = (acc_ref[h] / l_ref[h]).T.astype(o_ref.dtype)


def _mla_prompt_attention(q, kv, batch, seq, tq):
    nq = seq // tq
    hb = FLASH_HEADS
    v_block0 = MLA_HEADS * HEAD_PAD // (hb * V_DIM)
    return pl.pallas_call(
        functools.partial(_flash_kernel, tq=tq),
        grid=(batch, MLA_HEADS // hb, nq),
        in_specs=[
            pl.BlockSpec((tq, hb * HEAD_PAD), lambda b, h, i: (b * nq + i, h)),
            pl.BlockSpec((seq, hb * HEAD_PAD), lambda b, h, i: (b, h)),
            pl.BlockSpec((seq, hb * V_DIM), lambda b, h, i: (b, v_block0 + h)),
        ],
        out_specs=pl.BlockSpec((tq, hb * V_DIM), lambda b, h, i: (b * nq + i, h)),
        out_shape=jax.ShapeDtypeStruct((batch * seq, MLA_HEADS * V_DIM), BF16),
        scratch_shapes=[
            pltpu.VMEM((hb, 1, tq), F32),
            pltpu.VMEM((hb, 1, tq), F32),
            pltpu.VMEM((hb, V_DIM, tq), F32),
        ],
        compiler_params=_cparams("parallel", "parallel", "arbitrary"),
        name="mla_prompt_attention",
    )(q, kv, kv)


def _head_mm_kernel(a_ref, w_ref, o_ref):
    o_ref[...] = _dot(a_ref[...], w_ref[...]).astype(o_ref.dtype)


def _q_absorb(q_s, w_uk_t):
    db = q_s.shape[0]
    return pl.pallas_call(
        _head_mm_kernel,
        grid=(MLA_HEADS,),
        in_specs=[
            pl.BlockSpec((db, NOPE_DIM), lambda h: (0, 2 * h)),
            pl.BlockSpec((None, NOPE_DIM, KV_LORA), lambda h: (h, 0, 0)),
        ],
        out_specs=pl.BlockSpec((None, db, KV_LORA), lambda h: (h, 0, 0)),
        out_shape=jax.ShapeDtypeStruct((MLA_HEADS, db, KV_LORA), BF16),
        compiler_params=_cparams("parallel"),
        name="mla_q_absorb",
    )(q_s, w_uk_t)


def _v_up(o_lat_t, w_uv_t):
    db = o_lat_t.shape[1]
    return pl.pallas_call(
        _head_mm_kernel,
        grid=(MLA_HEADS,),
        in_specs=[
            pl.BlockSpec((None, db, KV_LORA), lambda h: (h, 0, 0)),
            pl.BlockSpec((None, KV_LORA, V_DIM), lambda h: (h, 0, 0)),
        ],
        out_specs=pl.BlockSpec((db, V_DIM), lambda h: (0, h)),
        out_shape=jax.ShapeDtypeStruct((db, MLA_HEADS * V_DIM), BF16),
        compiler_params=_cparams("parallel"),
        name="mla_v_up",
    )(o_lat_t, w_uv_t)


def _mla_decode_kernel(pt_ref, q_ref, new_ref, cache_ref, o_ref, pages_ref, kvt0_ref, kvt1_ref, s0_ref, s1_ref,
                       m_ref, l_ref, acc_ref, sem, *, n_pages, n_chunks, n_steps):
    kvt_refs = (kvt0_ref, kvt1_ref)
    s_refs = (s0_ref, s1_ref)
    t = pl.program_id(0)
    prev_chunk = (t + n_chunks - 1) % n_chunks

    def page_copy(step, p):
        chunk = jnp.minimum(step, n_steps - 2)
        slot = step % DEC_SLOTS
        return pltpu.make_async_copy(cache_ref.at[pt_ref[chunk * n_pages + p]], pages_ref.at[slot, p], sem.at[slot])

    def start_chunk(step):
        for p in range(n_pages):
            page_copy(step, p).start(priority=p % 2)

    def wait_chunk(step):
        slot = step % DEC_SLOTS
        pltpu.make_async_copy(cache_ref.at[pl.ds(0, n_pages)], pages_ref.at[slot], sem.at[slot]).wait()

    @pl.when(t == 0)
    def _():
        for step in range(DEC_SLOTS - 1):
            start_chunk(step)
        kvt_refs[1][...] = jnp.zeros(kvt_refs[1].shape, BF16)
        s_refs[1][...] = jnp.zeros(s_refs[1].shape, F32)
        m_ref[...] = jnp.zeros(m_ref.shape, F32)
        l_ref[...] = jnp.zeros(l_ref.shape, F32)
        acc_ref[...] = jnp.zeros(acc_ref.shape, F32)

    @pl.when(jnp.logical_and(t > 0, prev_chunk == 0))
    def _():
        new = new_ref[...].astype(BF16).astype(F32)
        m_ref[...] = jnp.sum(q_ref[...].astype(F32) * new, axis=-1, keepdims=True) * MLA_SCALE
        l_ref[...] = jnp.ones(l_ref.shape, F32)
        acc_ref[...] = jnp.broadcast_to(new, acc_ref.shape)

    def work(cur, prev):
        kvt_cur, kvt_prev = kvt_refs[cur], kvt_refs[prev]
        slot = t % DEC_SLOTS
        start_chunk(t + DEC_SLOTS - 1)
        wait_chunk(t)
        for p in range(n_pages):
            kvt_cur[:, p * PAGE_SIZE:(p + 1) * PAGE_SIZE] = pages_ref[slot, p].astype(BF16)
        s_refs[cur][...] = _dot(q_ref[...], kvt_cur[...]) * MLA_SCALE
        s = s_refs[prev][...]
        m_prev = m_ref[...]
        m_new = jnp.maximum(m_prev, jnp.max(s, axis=-1, keepdims=True))
        alpha = jnp.exp(m_prev - m_new)
        p = jnp.exp(s - m_new)
        l_ref[...] = alpha * l_ref[...] + jnp.sum(p, axis=-1, keepdims=True)
        acc_ref[...] = alpha * acc_ref[...] + _dot_nt(p.astype(BF16), kvt_prev[...])
        m_ref[...] = m_new

    @pl.when(t % 2 == 0)
    def _():
        work(0, 1)

    @pl.when(t % 2 == 1)
    def _():
        work(1, 0)

    @pl.when(jnp.logical_and(t > 0, prev_chunk == n_chunks - 1))
    def _():
        o_ref[...] = (acc_ref[...] / l_ref[...])[:, :KV_LORA]

    @pl.when(t == n_steps - 1)
    def _():
        for step in range(n_steps, n_steps + DEC_SLOTS - 1):
            wait_chunk(step)


def _mla_decode(q_cat, rows_new, cache_t, page_table):
    db = q_cat.shape[0]
    n_tab = page_table.shape[1]
    n_pages = min(DEC_PAGES, n_tab // 2)
    n_chunks = n_tab // n_pages
    assert n_chunks >= 2 and n_chunks * n_pages == n_tab
    n_steps = db * n_chunks + 1
    width = n_pages * PAGE_SIZE
    chunk = lambda t: jnp.minimum(t, n_steps - 2)
    batch_of = lambda t: chunk(t) // n_chunks

    grid_spec = pltpu.PrefetchScalarGridSpec(
        num_scalar_prefetch=1,
        grid=(n_steps,),
        in_specs=[
            pl.BlockSpec((None, MLA_HEADS, ROW_DIM), lambda t, pt: (batch_of(t), 0, 0)),
            pl.BlockSpec((None, 1, ROW_DIM), lambda t, pt: (batch_of(t), 0, 0)),
            pl.BlockSpec(memory_space=pl.ANY),
        ],
        out_specs=pl.BlockSpec((None, MLA_HEADS, KV_LORA),
                               lambda t, pt: (jnp.maximum(t - 1, 0) // n_chunks, 0, 0)),
        scratch_shapes=[
            pltpu.VMEM((DEC_SLOTS, n_pages, ROW_DIM, PAGE_SIZE), F32),
            pltpu.VMEM((ROW_DIM, width), BF16),
            pltpu.VMEM((ROW_DIM, width), BF16),
            pltpu.VMEM((MLA_HEADS, width), F32),
            pltpu.VMEM((MLA_HEADS, width), F32),
            pltpu.VMEM((MLA_HEADS, 1), F32),
            pltpu.VMEM((MLA_HEADS, 1), F32),
            pltpu.VMEM((MLA_HEADS, ROW_DIM), F32),
            pltpu.SemaphoreType.DMA((DEC_SLOTS,)),
        ],
    )
    return pl.pallas_call(
        functools.partial(_mla_decode_kernel, n_pages=n_pages, n_chunks=n_chunks, n_steps=n_steps),
        grid_spec=grid_spec,
        out_shape=jax.ShapeDtypeStruct((db, MLA_HEADS, KV_LORA), F32),
        compiler_params=_cparams("arbitrary"),
        name="mla_decode",
    )(page_table.reshape(-1), q_cat, rows_new.reshape(db, 1, ROW_DIM), cache_t)


def _dil_proj_kernel(x_ref, gq_ref, gkv_ref, wq_ref, wk_ref, wv_ref, qkv_ref, kv_ref, y_ref, *, dil):
    x = x_ref[...]
    xr = x * lax.rsqrt(jnp.mean(x * x, axis=-1, keepdims=True) + EPS)
    xq = (xr * gq_ref[...]).astype(BF16)
    xkv = (xr * gkv_ref[...]).astype(BF16)
    tm = x.shape[0]
    n = tm // dil
    tw = kv_ref.shape[0]
    for j, w_ref in enumerate((wq_ref, wk_ref, wv_ref)):
        y = _dot(xq if j == 0 else xkv, w_ref[...])
        if j > 0:
            kv_ref[:, (j - 1) * GROUP_DIM:j * GROUP_DIM] = y[tm - tw:, :]
        if dil == 1:
            qkv_ref[:, j * GROUP_DIM:(j + 1) * GROUP_DIM] = y.astype(BF16)
            continue
        for h in range(GROUP_HEADS):
            y_ref[h] = y[:, h * B_HEAD_DIM:(h + 1) * B_HEAD_DIM]
        for r in range(dil):
            for h in range(GROUP_HEADS):
                lo = r * QKV_DIM + j * GROUP_DIM + h * B_HEAD_DIM
                qkv_ref[:, lo:lo + B_HEAD_DIM] = y_ref[h, pl.ds(r, n, stride=dil), :].astype(BF16)


def _dil_proj(x, gq, gkv, w_q, w_kv, g, dil, window, seq, tm):
    m, d = x.shape
    tiles_per_seq = seq // tm
    tw = min(tm, window)
    win_tiles = window // tw
    assert window <= seq and window % tw == 0 and tm % tw == 0

    def win_block(i):
        return (i // tiles_per_seq) * win_tiles + jnp.maximum(i % tiles_per_seq - (tiles_per_seq - win_tiles), 0)

    w_block = (d, GROUP_DIM)
    return pl.pallas_call(
        functools.partial(_dil_proj_kernel, dil=dil),
        grid=(m // tm,),
        in_specs=[
            pl.BlockSpec((tm, d), lambda i: (i, 0)),
            pl.BlockSpec((1, d), lambda i: (0, 0)),
            pl.BlockSpec((1, d), lambda i: (0, 0)),
            pl.BlockSpec(w_block, lambda i: (0, g)),
            pl.BlockSpec(w_block, lambda i: (0, g)),
            pl.BlockSpec(w_block, lambda i: (0, N_GROUPS + g)),
        ],
        out_specs=[
            pl.BlockSpec((tm // dil, dil * QKV_DIM), lambda i: (i, 0)),
            pl.BlockSpec((tw, 2 * GROUP_DIM), lambda i: (win_block(i), 0)),
        ],
        out_shape=[
            jax.ShapeDtypeStruct((m // dil, dil * QKV_DIM), BF16),
            jax.ShapeDtypeStruct((m // seq * window, 2 * GROUP_DIM), F32),
        ],
        scratch_shapes=[pltpu.VMEM((GROUP_HEADS, tm, B_HEAD_DIM), F32)],
        compiler_params=_cparams("arbitrary"),
        name="dilated_proj",
    )(x, gq, gkv, w_q, w_kv, w_kv)


def _dil_prompt_kernel(q_ref, kc_ref, kp_ref, vc_ref, vp_ref, bias_ref, o_ref, lse_ref):
    first = pl.program_id(2) == 0
    heads = [slice(h * B_HEAD_DIM, (h + 1) * B_HEAD_DIM) for h in range(GROUP_HEADS)]
    scores = [_dot_nt(q_ref[:, hs], jnp.concatenate([kp_ref[:, hs], kc_ref[:, hs]], axis=0)) for hs in heads]
    for h, hs in enumerate(heads):
        v = jnp.concatenate([vp_ref[:, hs], vc_ref[:, hs]], axis=0)
        s = scores[h] * B_SCALE + bias_ref[h]
        col = lax.broadcasted_iota(jnp.int32, s.shape, 1)
        s = jnp.where(jnp.logical_and(first, col < DIL_TILE), NEG_INF, s)
        m = jnp.max(s, axis=-1, keepdims=True)
        p = jnp.exp(s - m)
        l = jnp.sum(p, axis=-1, keepdims=True)
        o_ref[:, hs] = _dot(p.astype(BF16), v) / l
        lse_ref[:, hs] = jnp.broadcast_to(m + jnp.log(l), (DIL_TILE, B_HEAD_DIM))


def _dil_prompt(qkv, bias, dil, batch, seq):
    sd = seq // dil
    nq = sd // DIL_TILE
    blk = (DIL_TILE, GROUP_DIM)
    cur = lambda b, i: b * nq + i
    prev = lambda b, i: b * nq + jnp.maximum(i - 1, 0)
    return pl.pallas_call(
        _dil_prompt_kernel,
        grid=(batch, dil, nq),
        in_specs=[
            pl.BlockSpec(blk, lambda b, r, i: (cur(b, i), 3 * r)),
            pl.BlockSpec(blk, lambda b, r, i: (cur(b, i), 3 * r + 1)),
            pl.BlockSpec(blk, lambda b, r, i: (prev(b, i), 3 * r + 1)),
            pl.BlockSpec(blk, lambda b, r, i: (cur(b, i), 3 * r + 2)),
            pl.BlockSpec(blk, lambda b, r, i: (prev(b, i), 3 * r + 2)),
            pl.BlockSpec((GROUP_HEADS, DIL_TILE, 2 * DIL_TILE), lambda b, r, i: (0, 0, 0)),
        ],
        out_specs=[
            pl.BlockSpec(blk, lambda b, r, i: (cur(b, i), r)),
            pl.BlockSpec(blk, lambda b, r, i: (cur(b, i), r)),
        ],
        out_shape=[
            jax.ShapeDtypeStruct((batch * sd, dil * GROUP_DIM), F32),
            jax.ShapeDtypeStruct((batch * sd, dil * GROUP_DIM), F32),
        ],
        compiler_params=_cparams("parallel", "parallel", "arbitrary"),
        name="dilated_prompt",
    )(qkv, qkv, qkv, qkv, qkv, bias)


def _dil_sample_kernel(q_ref, kn_ref, vn_ref, buf_ref, bias_ref, o_ref, lse_ref, *, dil):
    row = lax.broadcasted_iota(jnp.int32, (SUBLANES, B_HEAD_DIM), 0)
    row_wide = lax.broadcasted_iota(jnp.int32, (SUBLANES, GROUP_DIM), 0)
    own = row_wide == lax.broadcasted_iota(jnp.int32, (SUBLANES, GROUP_DIM), 1) // B_HEAD_DIM
    bias_keys = bias_ref[:, :DIL_KEYS]
    bias_new = bias_ref[:, DIL_KEYS:DIL_KEYS + 1]

    if dil == 1:
        rows_ref, stride = buf_ref, BUF_ROWS
    else:
        rows_ref, stride = buf_ref.reshape(DIL_SAMPLE_BLOCK, DIL_KEYS * BUF_ROWS_PAD, B_HEAD_DIM), BUF_ROWS_PAD

    def cached(b, r):
        return rows_ref[b, pl.ds(r, DIL_KEYS, stride=stride), :]

    for b in range(DIL_SAMPLE_BLOCK):
        qb = q_ref[b:b + 1, :]
        s = jnp.zeros((SUBLANES, DIL_KEYS), F32)
        for h in range(GROUP_HEADS):
            qh = jnp.broadcast_to(qb[:, h * B_HEAD_DIM:(h + 1) * B_HEAD_DIM], (SUBLANES, B_HEAD_DIM))
            sh = _dot_nt(qh.astype(BF16), cached(b, 2 * h).astype(BF16))
            s = jnp.where(row == h, sh, s)
        s = s * B_SCALE + bias_keys
        kn = kn_ref[b:b + 1, :].astype(BF16).astype(F32)
        vn = vn_ref[b:b + 1, :].astype(BF16).astype(F32)
        q_own = jnp.where(own, qb, 0.0).astype(BF16).astype(F32)
        s_new = jnp.sum(q_own * kn, axis=-1, keepdims=True) * B_SCALE + bias_new
        m = jnp.maximum(jnp.max(s, axis=-1, keepdims=True), s_new)
        p = jnp.exp(s - m)
        p_new = jnp.exp(s_new - m)
        l = jnp.sum(p, axis=-1, keepdims=True) + p_new
        p16 = p.astype(BF16)
        pn = p_new.astype(BF16).astype(F32)
        lse = m + jnp.log(l)
        for h in range(GROUP_HEADS):
            hs = slice(h * B_HEAD_DIM, (h + 1) * B_HEAD_DIM)
            oh = (_dot(p16, cached(b, 2 * h + 1).astype(BF16)) + pn * vn[:, hs]) / l
            o_ref[b:b + 1, hs] = jnp.sum(jnp.where(row == h, oh, 0.0), axis=0, keepdims=True)
            lse_ref[b:b + 1, hs] = jnp.sum(jnp.where(row == h, lse, 0.0), axis=0, keepdims=True)


def _dil_sample(q_s, kv_s, buf, bias, g, dil):
    db, length = buf.shape[:2]
    bb = DIL_SAMPLE_BLOCK
    rows = jnp.transpose(buf, (0, 1, 3, 2, 4))
    if dil == 1:
        rows = rows.reshape(db, length * BUF_ROWS, B_HEAD_DIM)
        buf_spec = pl.BlockSpec((bb, length * BUF_ROWS, B_HEAD_DIM), lambda i: (i, 0, 0))
    else:
        rows = rows.reshape(db, length // dil, dil * BUF_ROWS, B_HEAD_DIM)
        buf_spec = pl.BlockSpec((bb, DIL_KEYS, BUF_ROWS_PAD, B_HEAD_DIM), lambda i: (i, 0, 0, 0))
    return pl.pallas_call(
        functools.partial(_dil_sample_kernel, dil=dil),
        grid=(db // bb,),
        in_specs=[
            pl.BlockSpec((bb, GROUP_DIM), lambda i: (i, g)),
            pl.BlockSpec((bb, GROUP_DIM), lambda i: (i, g)),
            pl.BlockSpec((bb, GROUP_DIM), lambda i: (i, N_GROUPS + g)),
            buf_spec,
            pl.BlockSpec((SUBLANES, 2 * DIL_KEYS), lambda i: (0, 0)),
        ],
        out_specs=[
            pl.BlockSpec((bb, GROUP_DIM), lambda i: (i, 0)),
            pl.BlockSpec((bb, GROUP_DIM), lambda i: (i, 0)),
        ],
        out_shape=[
            jax.ShapeDtypeStruct((db, GROUP_DIM), F32),
            jax.ShapeDtypeStruct((db, GROUP_DIM), F32),
        ],
        compiler_params=_cparams("parallel"),
        name="dilated_sample",
    )(q_s, kv_s, kv_s, rows, bias)


def _merge_out_kernel(o0, o1, o2, l0, l1, l2, w_ref, x_ref, g_ref, out_ref, merged_ref, *scratch, dils):
    srcs = ((o0, l0), (o1, l1), (o2, l2))
    tm = x_ref.shape[0]
    ordered = []
    si = 0
    for (o_src, l_src), dil in zip(srcs, dils):
        if dil == 1:
            ordered.append(None)
            continue
        o_dst, l_dst = scratch[si], scratch[si + 1]
        si += 2
        n = tm // dil
        for r in range(dil):
            for h in range(GROUP_HEADS):
                lo = r * GROUP_DIM + h * B_HEAD_DIM
                o_dst[h, pl.ds(r, n, stride=dil), :] = o_src[:, lo:lo + B_HEAD_DIM]
                l_dst[h, pl.ds(r, n, stride=dil), :] = l_src[:, lo:lo + B_HEAD_DIM]
        ordered.append((o_dst, l_dst))
    for h in range(GROUP_HEADS):
        hs = slice(h * B_HEAD_DIM, (h + 1) * B_HEAD_DIM)
        os_, ls_ = [], []
        for (o_src, l_src), dst in zip(srcs, ordered):
            os_.append(o_src[:, hs] if dst is None else dst[0][h])
            ls_.append(l_src[:, hs] if dst is None else dst[1][h])
        mx = jnp.maximum(jnp.maximum(ls_[0], ls_[1]), ls_[2])
        e0, e1, e2 = jnp.exp(ls_[0] - mx), jnp.exp(ls_[1] - mx), jnp.exp(ls_[2] - mx)
        merged = (e0 * os_[0] + e1 * os_[1] + e2 * os_[2]) / (e0 + e1 + e2)
        merged_ref[:, hs] = merged.astype(BF16)
    y = _dot(merged_ref[...], w_ref[...])
    out_ref[...] = x_ref[...] + _rms(y, g_ref[...])


def _merge_out(outs, lses, dils, w, x, g, tm):
    m, d = x.shape
    parts = [pl.BlockSpec((tm // dil, dil * GROUP_DIM), lambda i: (i, 0)) for dil in dils]
    scratch = [pltpu.VMEM((tm, GROUP_DIM), BF16)]
    for dil in dils:
        if dil > 1:
            scratch += [pltpu.VMEM((GROUP_HEADS, tm, B_HEAD_DIM), F32)] * 2
    return pl.pallas_call(
        functools.partial(_merge_out_kernel, dils=tuple(dils)),
        grid=(m // tm,),
        in_specs=parts + parts + [
            pl.BlockSpec((GROUP_DIM, d), lambda i: (0, 0)),
            pl.BlockSpec((tm, d), lambda i: (i, 0)),
            pl.BlockSpec((1, d), lambda i: (0, 0)),
        ],
        out_specs=pl.BlockSpec((tm, d), lambda i: (i, 0)),
        out_shape=jax.ShapeDtypeStruct((m, d), F32),
        scratch_shapes=scratch,
        compiler_params=_cparams("parallel"),
        name="dilated_merge_out",
    )(*outs, *lses, w, x, g)


def _t5_bucket(dist):
    max_exact = T5_BUCKETS // 2
    d = np.maximum(dist, 1).astype(np.float32)
    large = max_exact + (np.log(d / max_exact) / math.log(T5_MAX_DIST / max_exact)
                         * (T5_BUCKETS - max_exact)).astype(np.int32)
    large = np.minimum(large, T5_BUCKETS - 1)
    return np.where(dist < max_exact, dist, large).astype(np.int32)


def _rows_static(table, idx):
    pieces, start = [], 0
    for i in range(1, len(idx) + 1):
        if i == len(idx) or idx[i] != idx[start]:
            row = table[int(idx[start]):int(idx[start]) + 1]
            pieces.append(jnp.broadcast_to(row, (i - start,) + table.shape[1:]))
            start = i
    return jnp.concatenate(pieces, axis=0)


def _group_bias(t5_bias, g):
    window, dil = DIL_GROUPS[g]
    buckets = _t5_bucket(dil * np.arange(window // dil + 1))
    return _rows_static(t5_bias[:, g * GROUP_HEADS:(g + 1) * GROUP_HEADS].astype(F32), buckets)


def _prompt_bias(bias):
    n = 2 * DIL_TILE
    heads = bias.shape[1]
    period = jnp.concatenate([bias[::-1], jnp.full((n - DIL_KEYS, heads), NEG_INF, F32)], axis=0)
    band = jnp.tile(period, (DIL_TILE, 1))[:DIL_TILE * n].reshape(DIL_TILE, n, heads)
    return jnp.transpose(band, (2, 0, 1))


def _sample_bias(bias):
    keys = bias[1:][::-1].T
    own = jnp.broadcast_to(bias[0][:, None], (GROUP_HEADS, DIL_KEYS))
    both = jnp.concatenate([keys, own], axis=1)
    return jnp.pad(both, ((0, SUBLANES - GROUP_HEADS), (0, 0)))


def _rope_tables(pos):
    half = ROPE_DIM // 2
    inv = ROPE_THETA ** (-jnp.arange(half, dtype=F32) / half)
    ang = pos.astype(F32)[:, None] * inv
    cos, sin = jnp.cos(ang), jnp.sin(ang)
    zero = jnp.zeros((pos.shape[0], LANES - ROPE_DIM), F32)
    return (jnp.concatenate([cos, cos, zero], axis=1), jnp.concatenate([-sin, sin, zero], axis=1))


def _tile(n, pref):
    return pref if n % pref == 0 else n


def kernel(x_prompt, x_sample, cache_mla, page_table, cache_b_win0, cache_b_win1, cache_b_win2,
           norm_gains, w_mla_in, g_q_a, g_kv_a, w_uq, w_uk, w_uv, w_mla_o,
           g_kv_b, w_kv_b, w_q_b, w_b_o, t5_bias, w_ffn_in, w_ffn_out):
    batch, seq, d = x_prompt.shape
    db = x_sample.shape[0]
    assert x_sample.shape[1] == 1 and DEPTH == 2 and N_A_LAYERS == 1
    win_bufs = (cache_b_win0, cache_b_win1, cache_b_win2)
    for gi, (window, dil) in enumerate(DIL_GROUPS):
        assert win_bufs[gi].shape[1] == window and window // dil == DIL_KEYS
        assert seq % (dil * DIL_TILE) == 0
    mp = batch * seq
    xp = x_prompt.reshape(mp, d)
    xs = x_sample.reshape(db, d)
    gains = norm_gains.reshape(DEPTH, 4, 1, d)
    tm_p = _tile(mp, 1024)
    tm_h = _tile(mp, 512)

    rope_cols = jnp.pad(w_mla_in[0][:, Q_LORA + KV_LORA:], ((0, 0), (0, LANES - ROPE_DIM)))
    w_in_pad = jnp.concatenate([w_mla_in[0][:, :Q_LORA + KV_LORA], rope_cols], axis=1).astype(BF16)
    w_uq_pad = jnp.pad(w_uq[0].reshape(Q_LORA, MLA_HEADS, NOPE_DIM + ROPE_DIM),
                       ((0, 0), (0, 0), (0, HEAD_PAD - NOPE_DIM - ROPE_DIM)))
    w_uq_pad = w_uq_pad.reshape(Q_LORA, MLA_HEADS * HEAD_PAD).astype(BF16)
    w_k_lat = jnp.pad(w_uk[0], ((0, 0), (0, 0), (0, HEAD_PAD - NOPE_DIM)))
    rope_copy = jnp.pad(jnp.eye(ROPE_DIM, dtype=F32), ((0, 0), (NOPE_DIM, HEAD_PAD - NOPE_DIM - ROPE_DIM)))
    rope_copy = jnp.broadcast_to(rope_copy[:, None, :], (ROPE_DIM, MLA_HEADS, HEAD_PAD))
    w_k = jnp.concatenate([w_k_lat, rope_copy], axis=0).reshape(ROW_DIM, MLA_HEADS * HEAD_PAD)
    w_v = jnp.pad(w_uv[0].reshape(KV_LORA, MLA_HEADS * V_DIM), ((0, ROPE_DIM), (0, 0)))
    w_kv = jnp.concatenate([w_k, w_v], axis=1).astype(BF16)
    w_uk_t = jnp.transpose(w_uk[0], (1, 2, 0)).astype(BF16)
    w_uv_t = jnp.transpose(w_uv[0], (1, 0, 2)).astype(BF16)
    w_o = w_mla_o[0].astype(BF16)
    w_kvb = w_kv_b.astype(BF16)
    w_qb = w_q_b[0].astype(BF16)
    w_bo = w_b_o[0].astype(BF16)
    w_f_in = w_ffn_in.astype(BF16)
    w_f_out = w_ffn_out.astype(BF16)
    f = w_ffn_out.shape[1]
    tf = _tile(f, 512)

    c_p, s_p = _rope_tables(jnp.tile(jnp.arange(seq), batch))
    c_s, s_s = _rope_tables(jnp.full((db,), PAST_LEN, jnp.int32))

    g = gains[0]
    cq_p, rows_p = _mla_in(xp, g[0], w_in_pad, g_q_a[0:1], g_kv_a[0:1], c_p, s_p, tm_h)
    cq_s, rows_s = _mla_in(xs, g[0], w_in_pad, g_q_a[0:1], g_kv_a[0:1], c_s, s_s, db)
    q_p = _q_up(cq_p, w_uq_pad, c_p, s_p, tm_p, 1024, out_scale=MLA_SCALE * math.log2(math.e))
    q_s = _q_up(cq_s, w_uq_pad, c_s, s_s, db, 1024)

    kv_p = _cast_mm(rows_p, w_kv, BF16, tm_p, 2048)
    attn_p = _mla_prompt_attention(q_p, kv_p, batch, seq, _tile(seq, 512))
    xp = _mm_norm_res(attn_p, w_o, xp, g[1], tm_h, w_o.shape[0])

    q_lat = _q_absorb(q_s, w_uk_t)
    q_rope_s = q_s.reshape(db, MLA_HEADS, HEAD_PAD)[:, :, NOPE_DIM:NOPE_DIM + ROPE_DIM]
    q_cat = jnp.concatenate([jnp.transpose(q_lat, (1, 0, 2)), q_rope_s], axis=-1)
    cache_t = jnp.swapaxes(cache_mla[0], 1, 2)
    o_lat = _mla_decode(q_cat, rows_s, cache_t, page_table)
    attn_s = _v_up(jnp.transpose(o_lat, (1, 0, 2)).astype(BF16), w_uv_t)
    xs = _mm_norm_res(attn_s, w_o, xs, g[1], db, 512)

    xp = _ffn(xp, g[2], g[3], w_f_in, w_f_out, 0, tm_h, tf)
    xs = _ffn(xs, g[2], g[3], w_f_in, w_f_out, 0, db, tf)

    g = gains[1]
    g_kvb = g_kv_b.reshape(1, d)
    kvb_s = _norm_mm(xs, g_kvb, w_kvb, F32, db, 512)
    qb_s = _norm_mm(xs, g[0], w_qb, F32, db, GROUP_DIM)
    outs_p, lses_p, outs_s, lses_s, kv_groups = [], [], [], [], []
    dils = [dil for _, dil in DIL_GROUPS]
    biases = [_group_bias(t5_bias, gi) for gi in range(N_GROUPS)]
    band_all = _prompt_bias(jnp.concatenate(biases, axis=1))
    for gi, (window, dil) in enumerate(DIL_GROUPS):
        qkv, kv_g = _dil_proj(xp, g[0], g_kvb, w_qb, w_kvb, gi, dil, window, seq, tm_h)
        kv_groups.append(kv_g)
        bias = biases[gi]
        o, lse = _dil_prompt(qkv, band_all[gi * GROUP_HEADS:(gi + 1) * GROUP_HEADS], dil, batch, seq)
        outs_p.append(o)
        lses_p.append(lse)
        o, lse = _dil_sample(qb_s, kvb_s, win_bufs[gi], _sample_bias(bias), gi, dil)
        outs_s.append(o)
        lses_s.append(lse)
    xp = _merge_out(outs_p, lses_p, dils, w_bo, xp, g[1], _tile(mp, 256))
    xs = _merge_out(outs_s, lses_s, [1] * N_GROUPS, w_bo, xs, g[1], db)

    xp = _ffn(xp, g[2], g[3], w_f_in, w_f_out, 1, tm_h, tf)
    xs = _ffn(xs, g[2], g[3], w_f_in, w_f_out, 1, db, tf)

    kv6_s = kvb_s.reshape(db, 1, 2, N_GROUPS, GROUP_HEADS, B_HEAD_DIM)
    wins = []
    for gi, (window, _) in enumerate(DIL_GROUPS):
        wins.append(kv_groups[gi].reshape(batch, window, 2, GROUP_HEADS, B_HEAD_DIM))
        wins.append(kv6_s[:, :, :, gi])
    return (xp.reshape(batch, seq, d), xs.reshape(db, 1, d),
            rows_p.reshape(1, batch, seq, ROW_DIM), rows_s.reshape(1, db, 1, ROW_DIM), *wins)
```
